```python
import math
import jax, jax.numpy as jnp
from jax import lax
import numpy as np

D_MODEL = 1024
BATCH = 2
SEQ = 8192
DEPTH = 4

N_MIXERS = 4
PLE_DIM = 256
ALPHA = (2.0 * DEPTH) ** 0.25
BETA = (8.0 * DEPTH) ** -0.25
LN_EPS = 1e-5

RG_WIDTH = D_MODEL
RG_BLOCK = 256
RG_BLOCKS = RG_WIDTH // RG_BLOCK
RG_CONV = 4
RG_C = 8.0

HG_HEADS = 8
HG_DK = D_MODEL // HG_HEADS
HG_DV = D_MODEL // HG_HEADS
HG_KDIM = HG_HEADS * HG_DK
HG_VDIM = HG_HEADS * HG_DV

RET_HEADS = 4
RET_DK = D_MODEL // RET_HEADS
RET_DV = 2 * D_MODEL // RET_HEADS
RET_KDIM = RET_HEADS * RET_DK
RET_VDIM = RET_HEADS * RET_DV
RET_CHUNK = 64
ROPE_BASE = 10000.0

GLA_HEADS = 4
GLA_DK = D_MODEL // 2 // GLA_HEADS
GLA_DV = D_MODEL // GLA_HEADS
GLA_KDIM = GLA_HEADS * GLA_DK
GLA_VDIM = GLA_HEADS * GLA_DV
GLA_RANK = 16
GLA_TAU = 16.0

GATE_CHUNK = 32

FFN_DENSE = 2816
N_EXPERTS = 8
TOP_K = 2
FFN_EXPERT = 3584

N_RGLRU = (DEPTH + 3) // 4
N_HGRN = (DEPTH + 2) // 4
N_RET = (DEPTH + 1) // 4
N_GLA = DEPTH // 4
N_DENSE = (DEPTH + 1) // 2
N_MOE = DEPTH // 2

kernel_name = "hybrid_interleaved_rglru_hgrn2_retnet_gla_moe_deepnorm"

F32 = jnp.float32


def layer_norm(x, g, b):
    xf = x.astype(F32)
    mu = jnp.mean(xf, -1, keepdims=True)
    var = jnp.mean(jnp.square(xf - mu), -1, keepdims=True)
    return ((xf - mu) * lax.rsqrt(var + LN_EPS) * g + b).astype(x.dtype)


def head_rms_norm(o):
    return o * lax.rsqrt(jnp.mean(jnp.square(o), -1, keepdims=True) + LN_EPS)


def head_group_norm(o):
    mu = jnp.mean(o, -1, keepdims=True)
    var = jnp.mean(jnp.square(o - mu), -1, keepdims=True)
    return (o - mu) * lax.rsqrt(var + LN_EPS)


def split_heads(t, h):
    b, s, c = t.shape
    return t.reshape(b, s, h, c // h).transpose(0, 2, 1, 3)


def merge_heads(t):
    b, h, s, d = t.shape
    return t.transpose(0, 2, 1, 3).reshape(b, s, h * d)


def rotary(t):
    s, d = t.shape[-2], t.shape[-1]
    inv = ROPE_BASE ** (-jnp.arange(0, d, 2, dtype=F32) / d)
    ang = jnp.arange(s, dtype=F32)[:, None] * inv[None, :]
    cos, sin = jnp.cos(ang), jnp.sin(ang)
    t = t.astype(F32)
    t1, t2 = t[..., : d // 2], t[..., d // 2:]
    return jnp.concatenate([t1 * cos - t2 * sin, t1 * sin + t2 * cos], -1)


def causal_depthwise_conv(x, w, b):
    k = w.shape[0]
    y = lax.conv_general_dilated(
        x, w[:, None, :], window_strides=(1,), padding=[(k - 1, 0)],
        dimension_numbers=("NWC", "WIO", "NWC"), feature_group_count=x.shape[-1])
    return y + b


def _linear_recurrence_combine(left, right):
    a1, b1 = left
    a2, b2 = right
    return a1 * a2, a2 * b1 + b2


def chunk_gated_linear_attention(q, k, v, log_g, chunk):
    q, k, v, log_g = (t.astype(F32) for t in (q, k, v, log_g))
    b_, h_, s_, dk = q.shape
    dv = v.shape[-1]
    n = s_ // chunk

    def to_chunks(t):
        return t.reshape(b_, h_, n, chunk, t.shape[-1]).transpose(2, 0, 1, 3, 4)

    causal = jnp.tril(jnp.ones((chunk, chunk), bool))

    def step(state, inp):
        qi, ki, vi, gi = inp
        cum = jnp.cumsum(gi, axis=2)
        ref = cum[:, :, chunk // 2: chunk // 2 + 1]
        last = cum[:, :, -1:]
        inter = jnp.einsum("bhcd,bhde->bhce", qi * jnp.exp(cum), state)
        scores = jnp.einsum("bhcd,bhsd->bhcs", qi * jnp.exp(cum - ref), ki * jnp.exp(ref - cum))
        intra = jnp.einsum("bhcs,bhse->bhce", jnp.where(causal, scores, 0.0), vi)
        new_state = (state * jnp.exp(jnp.swapaxes(last, -1, -2))
                     + jnp.einsum("bhcd,bhce->bhde", ki * jnp.exp(last - cum), vi))
        return new_state, inter + intra

    init = jnp.zeros((b_, h_, dk, dv), F32)
    _, out = lax.scan(step, init, tuple(map(to_chunks, (q, k, v, log_g))))
    return out.transpose(1, 2, 0, 3, 4).reshape(b_, h_, s_, dv)


def chunk_retention(q, k, v, log_gamma, chunk):
    q, k, v = (t.astype(F32) for t in (q, k, v))
    b_, h_, s_, dk = q.shape
    dv = v.shape[-1]
    n = s_ // chunk
    pos = jnp.arange(chunk, dtype=F32)
    lg = log_gamma[:, None]
    decay_q = jnp.exp(lg * (pos + 1.0))[None, :, :, None]
    decay_k = jnp.exp(lg * (chunk - 1.0 - pos))[None, :, :, None]
    rel = pos[:, None] - pos[None, :]
    dmat = jnp.where(rel >= 0, jnp.exp(lg[:, :, None] * jnp.maximum(rel, 0.0)), 0.0)[None]
    gamma_chunk = jnp.exp(lg * chunk)[None, :, :, None]

    def to_chunks(t):
        return t.reshape(b_, h_, n, chunk, t.shape[-1]).transpose(2, 0, 1, 3, 4)

    def step(state, inp):
        qi, ki, vi = inp
        inter = jnp.einsum("bhcd,bhde->bhce", qi, state) * decay_q
        intra = jnp.einsum("bhcs,bhse->bhce", jnp.einsum("bhcd,bhsd->bhcs", qi, ki) * dmat, vi)
        new_state = state * gamma_chunk + jnp.einsum("bhcd,bhce->bhde", ki * decay_k, vi)
        return new_state, inter + intra

    init = jnp.zeros((b_, h_, dk, dv), F32)
    _, out = lax.scan(step, init, tuple(map(to_chunks, (q, k, v))))
    return out.transpose(1, 2, 0, 3, 4).reshape(b_, h_, s_, dv)


def rglru_mixer(x, w_in, conv_w, conv_b, w_a, b_a, w_x, b_x, lam, w_out):
    b_, s_, _ = x.shape
    gate_br, rec_br = jnp.split(x @ w_in, 2, axis=-1)
    gate_br = jax.nn.gelu(gate_br, approximate=True)
    u = causal_depthwise_conv(rec_br, conv_w, conv_b)
    ub = u.reshape(b_, s_, RG_BLOCKS, RG_BLOCK)
    r = jax.nn.sigmoid(jnp.einsum("bsnc,ncd->bsnd", ub, w_a).reshape(b_, s_, RG_WIDTH) + b_a)
    i = jax.nn.sigmoid(jnp.einsum("bsnc,ncd->bsnd", ub, w_x).reshape(b_, s_, RG_WIDTH) + b_x)
    log_a = -RG_C * jax.nn.softplus(-lam.astype(F32)) * r.astype(F32)
    a = jnp.exp(log_a)
    inp = jnp.sqrt(-jnp.expm1(2.0 * log_a)) * (i.astype(F32) * u.astype(F32))
    _, h = lax.associative_scan(_linear_recurrence_combine, (a, inp), axis=1)
    return (h.astype(x.dtype) * gate_br) @ w_out


def hgrn2_mixer(x, w_in, lb, w_out):
    q, fz, i, g = jnp.split(x @ w_in, [HG_KDIM, 2 * HG_KDIM, 2 * HG_KDIM + HG_VDIM], axis=-1)
    q = jax.nn.silu(q)
    f = lb + (1.0 - lb) * jax.nn.sigmoid(fz.astype(F32))
    k = 1.0 - f
    o = chunk_gated_linear_attention(split_heads(q, HG_HEADS), split_heads(k, HG_HEADS),
                                     split_heads(i, HG_HEADS), split_heads(jnp.log(f), HG_HEADS),
                                     GATE_CHUNK)
    o = merge_heads(head_rms_norm(o)).astype(x.dtype) * jax.nn.silu(g)
    return o @ w_out


def retention_mixer(x, w_in, w_out):
    q, k, v, g = jnp.split(x @ w_in, [RET_KDIM, 2 * RET_KDIM, 2 * RET_KDIM + RET_VDIM], axis=-1)
    q = rotary(split_heads(q, RET_HEADS))
    k = rotary(split_heads(k, RET_HEADS)) * (RET_DK ** -0.5)
    log_gamma = jnp.log1p(-jnp.exp2(-5.0 - jnp.arange(RET_HEADS, dtype=F32)))
    o = chunk_retention(q, k, split_heads(v, RET_HEADS), log_gamma, RET_CHUNK)
    o = merge_heads(head_group_norm(o)).astype(x.dtype)
    return (jax.nn.silu(g) * o) @ w_out


def gla_mixer(x, w_in, w_gate, b_gate, w_out):
    q, k, v, r, gl = jnp.split(
        x @ w_in, [GLA_KDIM, 2 * GLA_KDIM, 2 * GLA_KDIM + GLA_VDIM, 2 * GLA_KDIM + 2 * GLA_VDIM], axis=-1)
    log_alpha = jax.nn.log_sigmoid((gl @ w_gate + b_gate).astype(F32)) / GLA_TAU
    q = q * (GLA_DK ** -0.5)
    o = chunk_gated_linear_attention(split_heads(q, GLA_HEADS), split_heads(k, GLA_HEADS),
                                     split_heads(v, GLA_HEADS), split_heads(log_alpha, GLA_HEADS),
                                     GATE_CHUNK)
    o = merge_heads(head_rms_norm(o)).astype(x.dtype) * jax.nn.silu(r)
    return o @ w_out


def swiglu(x, w_in, w_out):
    g, u = jnp.split(x @ w_in, 2, axis=-1)
    return (jax.nn.silu(g) * u) @ w_out


def moe_swiglu(x, w_router, w_in, w_out):
    logits = (x @ w_router).astype(F32)
    top_vals, top_idx = lax.top_k(logits, TOP_K)
    weights = jax.nn.softmax(top_vals, axis=-1)
    gates = jnp.sum(jax.nn.one_hot(top_idx, N_EXPERTS, dtype=F32) * weights[..., None], axis=-2)
    y = jnp.zeros_like(x)
    for e in range(N_EXPERTS):
        y = y + gates[..., e:e + 1].astype(x.dtype) * swiglu(x, w_in[e], w_out[e])
    return y


def _nrm(key, shape, fan_in, scale=1.0):
    return jax.random.normal(key, shape, F32) * (scale * fan_in ** -0.5)


def setup_inputs(seed: int = 0) -> dict:
    key = jax.random.key(seed)
    ks = iter(jax.random.split(key, 48))
    D = D_MODEL
    small = lambda shape, s=0.01: s * jax.random.normal(next(ks), shape, F32)
    x = jax.random.normal(next(ks), (BATCH, SEQ, D), F32)
    p = jax.random.normal(next(ks), (DEPTH, BATCH, SEQ, PLE_DIM), F32)
    rg_w_in = _nrm(next(ks), (N_RGLRU, D, 2 * RG_WIDTH), D)
    rg_conv_w = _nrm(next(ks), (N_RGLRU, RG_CONV, RG_WIDTH), RG_CONV)
    rg_conv_b = small((N_RGLRU, RG_WIDTH))
    rg_w_a = _nrm(next(ks), (N_RGLRU, RG_BLOCKS, RG_BLOCK, RG_BLOCK), RG_BLOCK)
    rg_b_a = small((N_RGLRU, RG_WIDTH))
    rg_w_x = _nrm(next(ks), (N_RGLRU, RG_BLOCKS, RG_BLOCK, RG_BLOCK), RG_BLOCK)
    rg_b_x = small((N_RGLRU, RG_WIDTH))
    a_pow_c = jax.random.uniform(next(ks), (N_RGLRU, RG_WIDTH), F32, 0.9, 0.999)
    log_a = jnp.log(a_pow_c) / RG_C
    rg_lambda = log_a - jnp.log(-jnp.expm1(log_a))
    rg_w_out = _nrm(next(ks), (N_RGLRU, RG_WIDTH, D), RG_WIDTH, BETA)
    hg_w_in = _nrm(next(ks), (N_HGRN, D, 2 * HG_KDIM + 2 * HG_VDIM), D)
    hg_lb_logits = small((DEPTH, HG_KDIM), 0.5)
    hg_w_out = _nrm(next(ks), (N_HGRN, HG_VDIM, D), HG_VDIM, BETA)
    ret_w_in = _nrm(next(ks), (N_RET, D, 2 * RET_KDIM + 2 * RET_VDIM), D)
    ret_w_out = _nrm(next(ks), (N_RET, RET_VDIM, D), RET_VDIM, BETA)
    gla_w_in = _nrm(next(ks), (N_GLA, D, 2 * GLA_KDIM + 2 * GLA_VDIM + GLA_RANK), D)
    gla_w_gate = _nrm(next(ks), (N_GLA, GLA_RANK, GLA_KDIM), GLA_RANK)
    gla_b_gate = small((N_GLA, GLA_KDIM), 0.1)
    gla_w_out = _nrm(next(ks), (N_GLA, GLA_VDIM, D), GLA_VDIM, BETA)
    dense_w_in = _nrm(next(ks), (N_DENSE, D, 2 * FFN_DENSE), D)
    dense_w_out = _nrm(next(ks), (N_DENSE, FFN_DENSE, D), FFN_DENSE, BETA)
    moe_w_router = _nrm(next(ks), (N_MOE, D, N_EXPERTS), D)
    moe_w_in = _nrm(next(ks), (N_MOE, N_EXPERTS, D, 2 * FFN_EXPERT), D)
    moe_w_out = _nrm(next(ks), (N_MOE, N_EXPERTS, FFN_EXPERT, D), FFN_EXPERT, BETA)
    ple_w_proj = _nrm(next(ks), (DEPTH, PLE_DIM, D), PLE_DIM)
    ple_w_gate = _nrm(next(ks), (DEPTH, D, D), D)
    ln_mix_g = 1.0 + small((DEPTH, D))
    ln_mix_b = small((DEPTH, D))
    ln_ffn_g = 1.0 + small((DEPTH, D))
    ln_ffn_b = small((DEPTH, D))
    return {
        "x": x, "p": p,
        "rg_w_in": rg_w_in, "rg_conv_w": rg_conv_w, "rg_conv_b": rg_conv_b,
        "rg_w_a": rg_w_a, "rg_b_a": rg_b_a, "rg_w_x": rg_w_x, "rg_b_x": rg_b_x,
        "rg_lambda": rg_lambda, "rg_w_out": rg_w_out,
        "hg_w_in": hg_w_in, "hg_lb_logits": hg_lb_logits, "hg_w_out": hg_w_out,
        "ret_w_in": ret_w_in, "ret_w_out": ret_w_out,
        "gla_w_in": gla_w_in, "gla_w_gate": gla_w_gate, "gla_b_gate": gla_b_gate, "gla_w_out": gla_w_out,
        "dense_w_in": dense_w_in, "dense_w_out": dense_w_out,
        "moe_w_router": moe_w_router, "moe_w_in": moe_w_in, "moe_w_out": moe_w_out,
        "ple_w_proj": ple_w_proj, "ple_w_gate": ple_w_gate,
        "ln_mix_g": ln_mix_g, "ln_mix_b": ln_mix_b, "ln_ffn_g": ln_ffn_g, "ln_ffn_b": ln_ffn_b,
    }


def reference(x, p,
              rg_w_in, rg_conv_w, rg_conv_b, rg_w_a, rg_b_a, rg_w_x, rg_b_x, rg_lambda, rg_w_out,
              hg_w_in, hg_lb_logits, hg_w_out,
              ret_w_in, ret_w_out,
              gla_w_in, gla_w_gate, gla_b_gate, gla_w_out,
              dense_w_in, dense_w_out,
              moe_w_router, moe_w_in, moe_w_out,
              ple_w_proj, ple_w_gate,
              ln_mix_g, ln_mix_b, ln_ffn_g, ln_ffn_b):
    lb_sm = jax.nn.softmax(hg_lb_logits.astype(F32), axis=0)
    lower_bounds = jnp.cumsum(lb_sm, axis=0) - lb_sm[0:1]
    for i in range(DEPTH):
        kind, j = i % N_MIXERS, i // N_MIXERS
        if kind == 0:
            h = rglru_mixer(x, rg_w_in[j], rg_conv_w[j], rg_conv_b[j], rg_w_a[j], rg_b_a[j],
                            rg_w_x[j], rg_b_x[j], rg_lambda[j], rg_w_out[j])
        elif kind == 1:
            h = hgrn2_mixer(x, hg_w_in[j], lower_bounds[i], hg_w_out[j])
        elif kind == 2:
            h = retention_mixer(x, ret_w_in[j], ret_w_out[j])
        else:
            h = gla_mixer(x, gla_w_in[j], gla_w_gate[j], gla_b_gate[j], gla_w_out[j])
        x = layer_norm(ALPHA * x + h, ln_mix_g[i], ln_mix_b[i])
        if i % 2 == 0:
            f = swiglu(x, dense_w_in[i // 2], dense_w_out[i // 2])
        else:
            f = moe_swiglu(x, moe_w_router[i // 2], moe_w_in[i // 2], moe_w_out[i // 2])
        x = layer_norm(ALPHA * x + f, ln_ffn_g[i], ln_ffn_b[i])
        x = x + (p[i] @ ple_w_proj[i]) * jax.nn.sigmoid(x @ ple_w_gate[i])
    return x
```

```python
import functools
import math

import jax
import jax.numpy as jnp
from jax import lax
from jax.experimental import pallas as pl
from jax.experimental.pallas import tpu as pltpu

F32 = jnp.float32
BF16 = jnp.bfloat16
I32 = jnp.int32

D_MODEL = 1024
DEPTH = 4
ALPHA = (2.0 * DEPTH) ** 0.25
LN_EPS = 1e-5
RG_WIDTH = D_MODEL
RG_BLOCK = 256
RG_BLOCKS = RG_WIDTH // RG_BLOCK
RG_CONV = 4
RG_C = 8.0
HG_HEADS = 8
HG_DK = D_MODEL // HG_HEADS
HG_DV = D_MODEL // HG_HEADS
HG_KDIM = HG_HEADS * HG_DK
HG_VDIM = HG_HEADS * HG_DV
RET_HEADS = 4
RET_DK = D_MODEL // RET_HEADS
RET_DV = 2 * D_MODEL // RET_HEADS
RET_KDIM = RET_HEADS * RET_DK
RET_VDIM = RET_HEADS * RET_DV
ROPE_BASE = 10000.0
GLA_HEADS = 4
GLA_DK = D_MODEL // 2 // GLA_HEADS
GLA_DV = D_MODEL // GLA_HEADS
GLA_KDIM = GLA_HEADS * GLA_DK
GLA_VDIM = GLA_HEADS * GLA_DV
GLA_RANK = 16
GLA_TAU = 16.0
GATE_CHUNK = 32
N_EXPERTS = 8
FFN_EXPERT = 3584
PLE_DIM = 256

LANES = 128
SUBLANES = 8
VMEM_LIMIT = 56 << 20

ROW_TILE = 512
RG_TILE = 512
GLA_TILE = 256
RET_CHUNK = 256
MOE_TILE = 512
MOE_FCHUNK = 512


def _dot(a, b):
    return jnp.dot(a, b, preferred_element_type=F32)


def _dot_nt(a, b):
    return lax.dot_general(a, b, (((1,), (1,)), ((), ())), preferred_element_type=F32)


def _dot_tn(a, b):
    return lax.dot_general(a, b, (((0,), (0,)), ((), ())), preferred_element_type=F32)


def _sigmoid(v):
    return jax.nn.sigmoid(v)


def _silu(v):
    return v * jax.nn.sigmoid(v)


def _softplus(v):
    return jnp.maximum(v, 0.0) + jnp.log1p(jnp.exp(-jnp.abs(v)))


def _layer_norm(v, g, b):
    mu = jnp.mean(v, -1, keepdims=True)
    d = v - mu
    var = jnp.mean(d * d, -1, keepdims=True)
    return d * lax.rsqrt(var + LN_EPS) * g + b


def _params(sem):
    return pltpu.CompilerParams(dimension_semantics=sem, vmem_limit_bytes=VMEM_LIMIT)


def _full_spec(a, n_grid):
    nd = a.ndim
    if n_grid == 1:
        return pl.BlockSpec(a.shape, lambda i: (0,) * nd)
    return pl.BlockSpec(a.shape, lambda i, j: (0,) * nd)


def _row_call(body, row_ins, full_ins, outs, name, tm=ROW_TILE):
    t = row_ins[0].shape[0]
    assert t % tm == 0
    in_specs = [pl.BlockSpec((tm, a.shape[1]), lambda i: (i, 0)) for a in row_ins]
    in_specs += [_full_spec(a, 1) for a in full_ins]
    out_shape = [jax.ShapeDtypeStruct((t, c), dt) for c, dt in outs]
    out_specs = [pl.BlockSpec((tm, c), lambda i: (i, 0)) for c, dt in outs]
    return pl.pallas_call(
        body, grid=(t // tm,), in_specs=in_specs, out_specs=out_specs, out_shape=out_shape,
        compiler_params=_params(("arbitrary",)), name=name)(*row_ins, *full_ins)


def _rg_body(x_ref, win_ref, cw_ref, cb_ref, wa_ref, ba_ref, wx_ref, bx_ref, lam_ref, wout_ref,
             g_ref, b_ref, o_ref, prev_ref, hc_ref, a_ref, i_ref, h_ref):
    tc = x_ref.shape[0]
    w = RG_WIDTH

    @pl.when(pl.program_id(1) == 0)
    def _():
        prev_ref[...] = jnp.zeros_like(prev_ref)
        hc_ref[...] = jnp.zeros_like(hc_ref)

    x = x_ref[...]
    y = _dot(x.astype(BF16), win_ref[...])
    gate = jax.nn.gelu(y[:, :w], approximate=True)
    rec = y[:, w:]

    ext = jnp.concatenate([prev_ref[...], rec], axis=0)
    cw = cw_ref[...]
    u = cb_ref[...] + cw[0:1, :] * ext[SUBLANES - 3:SUBLANES - 3 + tc, :]
    for j in range(1, RG_CONV):
        off = SUBLANES - (RG_CONV - 1) + j
        u = u + cw[j:j + 1, :] * ext[off:off + tc, :]
    prev_ref[...] = rec[tc - SUBLANES:, :]

    ra, rx = [], []
    for n in range(RG_BLOCKS):
        ub = u[:, n * RG_BLOCK:(n + 1) * RG_BLOCK].astype(BF16)
        ra.append(_dot(ub, wa_ref[n]))
        rx.append(_dot(ub, wx_ref[n]))
    r = _sigmoid(jnp.concatenate(ra, axis=1) + ba_ref[...])
    ig = _sigmoid(jnp.concatenate(rx, axis=1) + bx_ref[...])
    log_a = (-RG_C * _softplus(-lam_ref[...])) * r
    th = jnp.tanh(log_a)
    a_ref[...] = jnp.exp(log_a)
    i_ref[...] = jnp.sqrt(-2.0 * th / (1.0 - th)) * (ig * u)

    row = lax.broadcasted_iota(I32, (SUBLANES, w), 0)

    def group(gi, hc):
        r0 = pl.multiple_of(gi * SUBLANES, SUBLANES)
        a8 = a_ref[pl.ds(r0, SUBLANES), :]
        b8 = i_ref[pl.ds(r0, SUBLANES), :]
        for d in (1, 2, 4):
            m = row >= d
            a_sh = pltpu.roll(a8, d, 0)
            b_sh = pltpu.roll(b8, d, 0)
            b8 = jnp.where(m, a8 * b_sh + b8, b8)
            a8 = jnp.where(m, a8 * a_sh, a8)
        h8 = a8 * hc + b8
        h_ref[pl.ds(r0, SUBLANES), :] = h8
        return h8[SUBLANES - 1:SUBLANES, :]

    hc_ref[...] = lax.fori_loop(0, tc // SUBLANES, group, hc_ref[...])

    hg = (h_ref[...] * gate).astype(BF16)
    mix = _dot(hg, wout_ref[...])
    o_ref[...] = _layer_norm(ALPHA * x + mix, g_ref[...], b_ref[...])


def _rg_layer(x2d, bsz, seq, w_in, conv_w, conv_b, w_a, b_a, w_x, b_x, lam, w_out, ln_g, ln_b):
    tc = RG_TILE
    ns = seq // tc
    row = lambda v: v.reshape(1, -1)
    fulls = [w_in.astype(BF16), conv_w, row(conv_b), w_a.astype(BF16), row(b_a), w_x.astype(BF16),
             row(b_x), row(lam), w_out.astype(BF16), row(ln_g), row(ln_b)]
    tile = pl.BlockSpec((tc, D_MODEL), lambda b, s: (b * ns + s, 0))
    return pl.pallas_call(
        _rg_body, grid=(bsz, ns),
        in_specs=[tile] + [_full_spec(a, 2) for a in fulls],
        out_specs=tile, out_shape=jax.ShapeDtypeStruct(x2d.shape, F32),
        scratch_shapes=[pltpu.VMEM((SUBLANES, RG_WIDTH), F32), pltpu.VMEM((1, RG_WIDTH), F32),
                        pltpu.VMEM((tc, RG_WIDTH), F32), pltpu.VMEM((tc, RG_WIDTH), F32),
                        pltpu.VMEM((tc, RG_WIDTH), F32)],
        compiler_params=_params(("arbitrary", "arbitrary")), name="rglru_mixer")(x2d, *fulls)


def _ple(x2, p_ref, wp_ref, wgate_ref):
    gate = _sigmoid(_dot(x2.astype(BF16), wgate_ref[...]))
    return x2 + _dot(p_ref[...].astype(BF16), wp_ref[...]) * gate


def _dense_ffn_body(x_ref, p_ref, wg_ref, wu_ref, wo_ref, g_ref, b_ref, wp_ref, wgate_ref, o_ref):
    x = x_ref[...]
    xb = x.astype(BF16)
    h = (_silu(_dot(xb, wg_ref[...])) * _dot(xb, wu_ref[...])).astype(BF16)
    x2 = _layer_norm(ALPHA * x + _dot(h, wo_ref[...]), g_ref[...], b_ref[...])
    o_ref[...] = _ple(x2, p_ref, wp_ref, wgate_ref)


def _dense_ffn_layer(x2d, p2d, w_in, w_out, ln_g, ln_b, w_proj, w_gate):
    f = w_out.shape[0]
    w_in = w_in.astype(BF16)
    fulls = [w_in[:, :f], w_in[:, f:], w_out.astype(BF16), ln_g.reshape(1, -1), ln_b.reshape(1, -1),
             w_proj.astype(BF16), w_gate.astype(BF16)]
    return _row_call(_dense_ffn_body, [x2d, p2d], fulls, [(D_MODEL, F32)], "dense_swiglu_ple", tm=256)[0]


def _out_ln_body(o_ref, x_ref, w_ref, g_ref, b_ref, y_ref):
    y_ref[...] = _layer_norm(ALPHA * x_ref[...] + _dot(o_ref[...], w_ref[...]), g_ref[...], b_ref[...])


def _out_ln(o2d, x2d, w_out, ln_g, ln_b):
    fulls = [w_out.astype(BF16), ln_g.reshape(1, -1), ln_b.reshape(1, -1)]
    return _row_call(_out_ln_body, [o2d, x2d], fulls, [(D_MODEL, F32)], "mixer_out_postnorm")[0]


def _gla_body(q_ref, k_ref, v_ref, lg_ref, gate_ref, o_ref, cum_ref, st_ref, *, heads, dk, dv):
    bsz, tb, _ = q_ref.shape
    c = GATE_CHUNK

    @pl.when(pl.program_id(0) == 0)
    def _():
        st_ref[...] = jnp.zeros_like(st_ref)

    rmod = lax.broadcasted_iota(I32, (tb, heads * dk), 0) % c
    for b in range(bsz):
        cum = lg_ref[b]
        d = 1
        while d < c:
            cum = cum + jnp.where(rmod >= d, pltpu.roll(cum, d, 0), 0.0)
            d *= 2
        cum_ref[b] = cum

    ti = lax.broadcasted_iota(I32, (c, c), 0)
    si = lax.broadcasted_iota(I32, (c, c), 1)
    causal = ti >= si

    def chunk(ci, carry):
        r0 = pl.multiple_of(ci * c, c)
        rows = pl.ds(r0, c)
        for b in range(bsz):
            for h in range(heads):
                kc = slice(h * dk, (h + 1) * dk)
                vc = slice(h * dv, (h + 1) * dv)
                cum = cum_ref[b, rows, kc]
                ref = cum[c // 2:c // 2 + 1, :]
                last = cum[c - 1:c, :]
                q = q_ref[b, rows, kc].astype(F32)
                k = k_ref[b, rows, kc].astype(F32)
                v = v_ref[b, rows, vc]
                st = st_ref[b, h]
                inter = _dot_nt((q * jnp.exp(cum)).astype(BF16), st.astype(BF16))
                sc = _dot_nt((q * jnp.exp(cum - ref)).astype(BF16),
                             (k * jnp.exp(ref - cum)).astype(BF16))
                sc = jnp.where(causal, sc, 0.0)
                o = inter + _dot(sc.astype(BF16), v)
                st_ref[b, h] = st * jnp.exp(last) + _dot_tn(v, (k * jnp.exp(last - cum)).astype(BF16))
                o = o * lax.rsqrt(jnp.mean(o * o, -1, keepdims=True) + LN_EPS)
                o_ref[b, rows, vc] = (o * gate_ref[b, rows, vc].astype(F32)).astype(BF16)
        return carry

    lax.fori_loop(0, tb // c, chunk, 0)


def _gla_core(q, k, v, lg, gate, bsz, seq, heads, dk, dv):
    tb = GLA_TILE
    r3 = lambda a: a.reshape(bsz, seq, a.shape[-1])
    ins = [r3(q), r3(k), r3(v), r3(lg), r3(gate)]
    spec = lambda a: pl.BlockSpec((bsz, tb, a.shape[-1]), lambda s: (0, s, 0))
    out = pl.pallas_call(
        functools.partial(_gla_body, heads=heads, dk=dk, dv=dv), grid=(seq // tb,),
        in_specs=[spec(a) for a in ins], out_specs=spec(ins[2]),
        out_shape=jax.ShapeDtypeStruct((bsz, seq, heads * dv), BF16),
        scratch_shapes=[pltpu.VMEM((bsz, tb, heads * dk), F32), pltpu.VMEM((bsz, heads, dv, dk), F32)],
        compiler_params=_params(("arbitrary",)), name="gated_linear_attention")(*ins)
    return out.reshape(bsz * seq, heads * dv)


def _hg_in_body(x_ref, w_ref, lb_ref, q_ref, k_ref, v_ref, lg_ref, gs_ref):
    y = _dot(x_ref[...].astype(BF16), w_ref[...])
    lb = lb_ref[...]
    f = lb + (1.0 - lb) * _sigmoid(y[:, HG_KDIM:2 * HG_KDIM])
    q_ref[...] = _silu(y[:, :HG_KDIM]).astype(BF16)
    k_ref[...] = (1.0 - f).astype(BF16)
    lg_ref[...] = jnp.log(f)
    v_ref[...] = y[:, 2 * HG_KDIM:2 * HG_KDIM + HG_VDIM].astype(BF16)
    gs_ref[...] = _silu(y[:, 2 * HG_KDIM + HG_VDIM:]).astype(BF16)


def _hgrn2_layer(x2d, bsz, seq, w_in, lb, w_out, ln_g, ln_b):
    outs = [(HG_KDIM, BF16), (HG_KDIM, BF16), (HG_VDIM, BF16), (HG_KDIM, F32), (HG_VDIM, BF16)]
    q, k, v, lg, gs = _row_call(_hg_in_body, [x2d], [w_in.astype(BF16), lb.reshape(1, -1)], outs,
                                "hgrn2_in_proj")
    o = _gla_core(q, k, v, lg, gs, bsz, seq, HG_HEADS, HG_DK, HG_DV)
    return _out_ln(o, x2d, w_out, ln_g, ln_b)


def _gla_in_body(x_ref, w_ref, wl_ref, wg_ref, bg_ref, q_ref, k_ref, v_ref, lg_ref, gs_ref):
    xb = x_ref[...].astype(BF16)
    y = _dot(xb, w_ref[...])
    gl = _dot(xb, wl_ref[...])
    z = _dot(gl.astype(BF16), wg_ref[...]) + bg_ref[...]
    lg_ref[...] = (jnp.minimum(z, 0.0) - jnp.log1p(jnp.exp(-jnp.abs(z)))) * (1.0 / GLA_TAU)
    q_ref[...] = (y[:, :GLA_KDIM] * (GLA_DK ** -0.5)).astype(BF16)
    k_ref[...] = y[:, GLA_KDIM:2 * GLA_KDIM].astype(BF16)
    v_ref[...] = y[:, 2 * GLA_KDIM:2 * GLA_KDIM + GLA_VDIM].astype(BF16)
    gs_ref[...] = _silu(y[:, 2 * GLA_KDIM + GLA_VDIM:]).astype(BF16)


def _gla_layer(x2d, bsz, seq, w_in, w_gate, b_gate, w_out, ln_g, ln_b):
    n_main = 2 * GLA_KDIM + 2 * GLA_VDIM
    w_main = w_in[:, :n_main].astype(BF16)
    w_low = jnp.pad(w_in[:, n_main:], ((0, 0), (0, LANES - GLA_RANK))).astype(BF16)
    w_gate_p = jnp.pad(w_gate, ((0, LANES - GLA_RANK), (0, 0))).astype(BF16)
    outs = [(GLA_KDIM, BF16), (GLA_KDIM, BF16), (GLA_VDIM, BF16), (GLA_KDIM, F32), (GLA_VDIM, BF16)]
    q, k, v, lg, gs = _row_call(_gla_in_body, [x2d], [w_main, w_low, w_gate_p, b_gate.reshape(1, -1)],
                                outs, "gla_in_proj")
    o = _gla_core(q, k, v, lg, gs, bsz, seq, GLA_HEADS, GLA_DK, GLA_DV)
    return _out_ln(o, x2d, w_out, ln_g, ln_b)


def _ret_in_body(x_ref, cos_ref, sin_ref, w_ref, q_ref, k_ref, v_ref, gs_ref):
    y = _dot(x_ref[...].astype(BF16), w_ref[...])
    cos = cos_ref[...]
    sin = sin_ref[...]
    half = RET_DK // 2

    def rot(base, scale, dst):
        for h in range(RET_HEADS):
            t1 = y[:, base + h * RET_DK:base + h * RET_DK + half]
            t2 = y[:, base + h * RET_DK + half:base + (h + 1) * RET_DK]
            dst[:, h * RET_DK:h * RET_DK + half] = ((t1 * cos - t2 * sin) * scale).astype(BF16)
            dst[:, h * RET_DK + half:(h + 1) * RET_DK] = ((t1 * sin + t2 * cos) * scale).astype(BF16)

    rot(0, 1.0, q_ref)
    rot(RET_KDIM, RET_DK ** -0.5, k_ref)
    v_ref[...] = y[:, 2 * RET_KDIM:2 * RET_KDIM + RET_VDIM].astype(BF16)
    gs_ref[...] = _silu(y[:, 2 * RET_KDIM + RET_VDIM:]).astype(BF16)


def _ret_body(q_ref, k_ref, v_ref, gs_ref, dq_ref, dk_ref, dm_ref, gc_ref, o_ref, st_ref):
    bsz = q_ref.shape[0]

    @pl.when(pl.program_id(0) == 0)
    def _():
        st_ref[...] = jnp.zeros_like(st_ref)

    for b in range(bsz):
        for h in range(RET_HEADS):
            kc = slice(h * RET_DK, (h + 1) * RET_DK)
            vc = slice(h * RET_DV, (h + 1) * RET_DV)
            q = q_ref[b, :, kc]
            k = k_ref[b, :, kc]
            v = v_ref[b, :, vc]
            st = st_ref[b, h]
            inter = _dot(q, st.astype(BF16)) * dq_ref[h]
            sc = _dot_nt(q, k) * dm_ref[h]
            o = inter + _dot(sc.astype(BF16), v)
            kd = (k.astype(F32) * dk_ref[h]).astype(BF16)
            st_ref[b, h] = st * gc_ref[h] + _dot_tn(kd, v)
            mu = jnp.mean(o, -1, keepdims=True)
            dlt = o - mu
            o = dlt * lax.rsqrt(jnp.mean(dlt * dlt, -1, keepdims=True) + LN_EPS)
            o_ref[b, :, vc] = (gs_ref[b, :, vc].astype(F32) * o).astype(BF16)


def _ret_layer(x2d, bsz, seq, w_in, w_out, ln_g, ln_b):
    half = RET_DK // 2
    inv = ROPE_BASE ** (-jnp.arange(0, RET_DK, 2, dtype=F32) / RET_DK)
    ang = jnp.arange(seq, dtype=F32)[:, None] * inv[None, :]
    cos, sin = jnp.cos(ang), jnp.sin(ang)
    tm = ROW_TILE
    ns = seq // tm
    t = x2d.shape[0]
    w = w_in.astype(BF16)
    outs = [(RET_KDIM, BF16), (RET_KDIM, BF16), (RET_VDIM, BF16), (RET_VDIM, BF16)]
    q, k, v, gs = pl.pallas_call(
        _ret_in_body, grid=(t // tm,),
        in_specs=[pl.BlockSpec((tm, D_MODEL), lambda i: (i, 0)),
                  pl.BlockSpec((tm, half), lambda i: (i % ns, 0)),
                  pl.BlockSpec((tm, half), lambda i: (i % ns, 0)),
                  _full_spec(w, 1)],
        out_specs=[pl.BlockSpec((tm, c), lambda i: (i, 0)) for c, _ in outs],
        out_shape=[jax.ShapeDtypeStruct((t, c), dt) for c, dt in outs],
        compiler_params=_params(("arbitrary",)), name="retention_in_proj")(x2d, cos, sin, w)

    c = RET_CHUNK
    pos = jnp.arange(c, dtype=F32)
    lgam = jnp.log1p(-jnp.exp2(-5.0 - jnp.arange(RET_HEADS, dtype=F32)))[:, None]
    decay_q = jnp.exp(lgam * (pos + 1.0))[:, :, None]
    decay_k = jnp.exp(lgam * (c - 1.0 - pos))[:, :, None]
    rel = pos[:, None] - pos[None, :]
    dmat = jnp.where(rel >= 0, jnp.exp(lgam[:, :, None] * jnp.maximum(rel, 0.0)), 0.0)
    gchunk = jnp.broadcast_to(jnp.exp(lgam * c)[:, :, None], (RET_HEADS, 1, LANES))
    decay_q = jnp.broadcast_to(decay_q, (RET_HEADS, c, LANES))
    decay_k = jnp.broadcast_to(decay_k, (RET_HEADS, c, LANES))
    r3 = lambda a: a.reshape(bsz, seq, a.shape[-1])
    ins = [r3(q), r3(k), r3(v), r3(gs)]
    tabs = [decay_q[:, :, :1], decay_k[:, :, :1], dmat, gchunk[:, :, :1]]
    spec = lambda a: pl.BlockSpec((bsz, c, a.shape[-1]), lambda s: (0, s, 0))
    o = pl.pallas_call(
        _ret_body, grid=(seq // c,),
        in_specs=[spec(a) for a in ins] + [_full_spec(a, 1) for a in tabs],
        out_specs=spec(ins[2]), out_shape=jax.ShapeDtypeStruct((bsz, seq, RET_VDIM), BF16),
        scratch_shapes=[pltpu.VMEM((bsz, RET_HEADS, RET_DK, RET_DV), F32)],
        compiler_params=_params(("arbitrary",)), name="retention")(*ins, *tabs)
    return _out_ln(o.reshape(t, RET_VDIM), x2d, w_out, ln_g, ln_b)


def _router_body(x_ref, wh_ref, wl_ref, meta_ref, cnt_ref, carry_ref):
    tm = x_ref.shape[0]

    @pl.when(pl.program_id(0) == 0)
    def _():
        carry_ref[...] = jnp.zeros_like(carry_ref)

    x = x_ref[...]
    xh = x.astype(BF16)
    xl = (x - xh.astype(F32)).astype(BF16)
    logits = _dot(xh, wh_ref[...]) + (_dot(xl, wh_ref[...]) + _dot(xh, wl_ref[...]))
    lane = lax.broadcasted_iota(I32, (tm, LANES), 1).astype(F32)
    neg = jnp.float32(-jnp.inf)
    big = jnp.float32(LANES)
    m0 = jnp.where(lane < N_EXPERTS, logits, neg)
    v1 = jnp.max(m0, -1, keepdims=True)
    i1 = jnp.min(jnp.where(m0 == v1, lane, big), -1, keepdims=True)
    m1 = jnp.where(lane == i1, neg, m0)
    v2 = jnp.max(m1, -1, keepdims=True)
    i2 = jnp.min(jnp.where(m1 == v2, lane, big), -1, keepdims=True)
    e2 = jnp.exp(v2 - v1)
    w1 = 1.0 / (1.0 + e2)
    w2 = e2 / (1.0 + e2)
    oh1 = (lane == i1).astype(F32)
    oh2 = (lane == i2).astype(F32)
    oh = oh1 + oh2
    ti = lax.broadcasted_iota(I32, (tm, tm), 0)
    si = lax.broadcasted_iota(I32, (tm, tm), 1)
    before = _dot((ti > si).astype(BF16), oh.astype(BF16)) + carry_ref[...]
    r1 = jnp.sum(before * oh1, -1, keepdims=True)
    r2 = jnp.sum(before * oh2, -1, keepdims=True)
    carry_ref[...] += jnp.sum(oh, 0, keepdims=True)
    cnt_ref[...] = jnp.broadcast_to(carry_ref[...], cnt_ref.shape)
    meta = jnp.zeros((tm, LANES), F32)
    for col, val in enumerate((i1, i2, w1, w2, r1, r2)):
        meta = jnp.where(lane == col, val, meta)
    meta_ref[...] = meta


def _moe_ffn_body(te_ref, nv_ref, src_ref, x_hbm, wg_ref, wu_ref, wo_ref, y_ref, xbuf, xs, acc, sem):
    tm = xbuf.shape[0]
    i = pl.program_id(0)
    j = pl.program_id(1)
    valid = i < nv_ref[0]

    def row_copy(r, tok):
        return pltpu.make_async_copy(x_hbm.at[pl.ds(tok, 1), :], xbuf.at[pl.ds(r, 1), :], sem)

    @pl.when(valid & (j == 0))
    def _():
        base = i * tm

        def issue(r, c):
            row_copy(r, src_ref[base + r]).start()
            return c

        def wait(r, c):
            row_copy(r, 0).wait()
            return c

        lax.fori_loop(0, tm, issue, 0)
        lax.fori_loop(0, tm, wait, 0)
        xs[...] = xbuf[...].astype(BF16)
        acc[...] = jnp.zeros_like(acc)

    @pl.when(valid)
    def _():
        xb = xs[...]
        h = (_silu(_dot(xb, wg_ref[...])) * _dot(xb, wu_ref[...])).astype(BF16)
        acc[...] += _dot(h, wo_ref[...])

    last = j == pl.num_programs(1) - 1

    @pl.when(valid & last)
    def _():
        y_ref[...] = acc[...]

    @pl.when(jnp.logical_not(valid) & last)
    def _():
        y_ref[...] = jnp.zeros_like(y_ref)


def _moe_combine_body(d1_ref, d2_ref, y_hbm, x_ref, meta_ref, p_ref, g_ref, b_ref, wp_ref, wgate_ref,
                      o_ref, y1, y2, sem):
    tm = x_ref.shape[0]
    base = pl.program_id(0) * tm

    def row_copy(buf, r, pos):
        return pltpu.make_async_copy(y_hbm.at[pl.ds(pos, 1), :], buf.at[pl.ds(r, 1), :], sem)

    def issue(r, c):
        row_copy(y1, r, d1_ref[base + r]).start()
        row_copy(y2, r, d2_ref[base + r]).start()
        return c

    def wait(r, c):
        row_copy(y1, r, 0).wait()
        row_copy(y2, r, 0).wait()
        return c

    lax.fori_loop(0, tm, issue, 0)
    lax.fori_loop(0, tm, wait, 0)
    meta = meta_ref[...]
    f = meta[:, 2:3] * y1[...] + meta[:, 3:4] * y2[...]
    x2 = _layer_norm(ALPHA * x_ref[...] + f, g_ref[...], b_ref[...])
    o_ref[...] = _ple(x2, p_ref, wp_ref, wgate_ref)


def _moe_layer(x2d, p2d, w_router, w_in, w_out, ln_g, ln_b, w_proj, w_gate):
    t = x2d.shape[0]
    tm = MOE_TILE
    tf = MOE_FCHUNK
    nf = FFN_EXPERT // tf

    wr = jnp.pad(w_router, ((0, 0), (0, LANES - N_EXPERTS)))
    wr_hi = wr.astype(BF16)
    wr_lo = (wr - wr_hi.astype(F32)).astype(BF16)
    rt = ROW_TILE
    meta, cnt = pl.pallas_call(
        _router_body, grid=(t // rt,),
        in_specs=[pl.BlockSpec((rt, D_MODEL), lambda i: (i, 0)), _full_spec(wr_hi, 1), _full_spec(wr_lo, 1)],
        out_specs=[pl.BlockSpec((rt, LANES), lambda i: (i, 0)), pl.BlockSpec((SUBLANES, LANES), lambda i: (0, 0))],
        out_shape=[jax.ShapeDtypeStruct((t, LANES), F32), jax.ShapeDtypeStruct((SUBLANES, LANES), F32)],
        scratch_shapes=[pltpu.VMEM((1, LANES), F32)],
        compiler_params=_params(("arbitrary",)), name="moe_router")(x2d, wr_hi, wr_lo)

    n_rows = 2 * t + N_EXPERTS * tm
    n_tiles = n_rows // tm
    counts = cnt[0, :N_EXPERTS].astype(I32)
    padded = ((counts + tm - 1) // tm) * tm
    ends = jnp.cumsum(padded)
    offs = ends - padded
    e1, e2 = meta[:, 0].astype(I32), meta[:, 1].astype(I32)
    d1 = offs[e1] + meta[:, 4].astype(I32)
    d2 = offs[e2] + meta[:, 5].astype(I32)
    tok = jnp.arange(t, dtype=I32)
    src = jnp.zeros((n_rows,), I32).at[d1].set(tok).at[d2].set(tok)
    n_valid = (ends[-1] // tm).astype(I32)
    starts = jnp.arange(n_tiles, dtype=I32) * tm
    te = jnp.minimum(jnp.sum(ends[None, :] <= starts[:, None], axis=1), N_EXPERTS - 1).astype(I32)
    te = jnp.where(jnp.arange(n_tiles) < n_valid, te, te[jnp.maximum(n_valid - 1, 0)])

    w_in = w_in.astype(BF16)
    w_out = w_out.astype(BF16)

    def fj(i, j, nv):
        return jnp.where(i < nv[0], j, nf - 1)

    y = pl.pallas_call(
        _moe_ffn_body,
        grid_spec=pltpu.PrefetchScalarGridSpec(
            num_scalar_prefetch=3, grid=(n_tiles, nf),
            in_specs=[pl.BlockSpec(memory_space=pl.ANY),
                      pl.BlockSpec((None, D_MODEL, tf), lambda i, j, te, nv, s: (te[i], 0, fj(i, j, nv))),
                      pl.BlockSpec((None, D_MODEL, tf), lambda i, j, te, nv, s: (te[i], 0, nf + fj(i, j, nv))),
                      pl.BlockSpec((None, tf, D_MODEL), lambda i, j, te, nv, s: (te[i], fj(i, j, nv), 0))],
            out_specs=pl.BlockSpec((tm, D_MODEL), lambda i, j, te, nv, s: (i, 0)),
            scratch_shapes=[pltpu.VMEM((tm, D_MODEL), F32), pltpu.VMEM((tm, D_MODEL), BF16),
                            pltpu.VMEM((tm, D_MODEL), F32), pltpu.SemaphoreType.DMA(())]),
        out_shape=jax.ShapeDtypeStruct((n_rows, D_MODEL), F32),
        compiler_params=_params(("arbitrary", "arbitrary")), name="moe_expert_ffn",
    )(te, n_valid.reshape(1), src, x2d, w_in, w_in, w_out)

    ct = 256
    fulls = [ln_g.reshape(1, -1), ln_b.reshape(1, -1), w_proj.astype(BF16), w_gate.astype(BF16)]
    return pl.pallas_call(
        _moe_combine_body,
        grid_spec=pltpu.PrefetchScalarGridSpec(
            num_scalar_prefetch=2, grid=(t // ct,),
            in_specs=[pl.BlockSpec(memory_space=pl.ANY),
                      pl.BlockSpec((ct, D_MODEL), lambda i, a, b: (i, 0)),
                      pl.BlockSpec((ct, LANES), lambda i, a, b: (i, 0)),
                      pl.BlockSpec((ct, PLE_DIM), lambda i, a, b: (i, 0))]
                     + [pl.BlockSpec(a.shape, lambda i, u, v, nd=a.ndim: (0,) * nd) for a in fulls],
            out_specs=pl.BlockSpec((ct, D_MODEL), lambda i, a, b: (i, 0)),
            scratch_shapes=[pltpu.VMEM((ct, D_MODEL), F32), pltpu.VMEM((ct, D_MODEL), F32),
                            pltpu.SemaphoreType.DMA(())]),
        out_shape=jax.ShapeDtypeStruct((t, D_MODEL), F32),
        compiler_params=_params(("arbitrary",)), name="moe_combine_postnorm_ple",
    )(d1, d2, y, x2d, meta, p2d, *fulls)


def kernel(x, p, rg_w_in, rg_conv_w, rg_conv_b, rg_w_a, rg_b_a, rg_w_x, rg_b_x, rg_lambda, rg_w_out,
           hg_w_in, hg_lb_logits, hg_w_out, ret_w_in, ret_w_out, gla_w_in, gla_w_gate, gla_b_gate,
           gla_w_out, dense_w_in, dense_w_out, moe_w_router, moe_w_in, moe_w_out, ple_w_proj,
           ple_w_gate, ln_mix_g, ln_mix_b, ln_ffn_g, ln_ffn_b):
    bsz, seq, d = x.shape
    t = bsz * seq
    depth = p.shape[0]
    lb_sm = jax.nn.softmax(hg_lb_logits.astype(F32), axis=0)
    lower_bounds = jnp.cumsum(lb_sm, axis=0) - lb_sm[0:1]
    h = x.reshape(t, d)
    for i in range(depth):
        kind, j = i % 4, i // 4
        if kind == 0:
            h = _rg_layer(h, bsz, seq, rg_w_in[j], rg_conv_w[j], rg_conv_b[j], rg_w_a[j], rg_b_a[j],
                          rg_w_x[j], rg_b_x[j], rg_lambda[j], rg_w_out[j], ln_mix_g[i], ln_mix_b[i])
        elif kind == 1:
            h = _hgrn2_layer(h, bsz, seq, hg_w_in[j], lower_bounds[i], hg_w_out[j], ln_mix_g[i], ln_mix_b[i])
        elif kind == 2:
            h = _ret_layer(h, bsz, seq, ret_w_in[j], ret_w_out[j], ln_mix_g[i], ln_mix_b[i])
        else:
            h = _gla_layer(h, bsz, seq, gla_w_in[j], gla_w_gate[j], gla_b_gate[j], gla_w_out[j],
                           ln_mix_g[i], ln_mix_b[i])
        p2d = p[i].reshape(t, p.shape[-1])
        if i % 2 == 0:
            h = _dense_ffn_layer(h, p2d, dense_w_in[i // 2], dense_w_out[i // 2], ln_ffn_g[i], ln_ffn_b[i],
                                 ple_w_proj[i], ple_w_gate[i])
        else:
            h = _moe_layer(h, p2d, moe_w_router[i // 2], moe_w_in[i // 2], moe_w_out[i // 2],
                           ln_ffn_g[i], ln_ffn_b[i], ple_w_proj[i], ple_w_gate[i])
    return h.reshape(bsz, seq, d)
```

```python
import functools
import math

import jax
import jax.numpy as jnp
from jax import lax
from jax.experimental import pallas as pl
from jax.experimental.pallas import tpu as pltpu

F32 = jnp.float32
BF16 = jnp.bfloat16
I32 = jnp.int32

D_MODEL = 1024
DEPTH = 4
ALPHA = (2.0 * DEPTH) ** 0.25
LN_EPS = 1e-5
RG_WIDTH = D_MODEL
RG_BLOCK = 256
RG_BLOCKS = RG_WIDTH // RG_BLOCK
RG_CONV = 4
RG_C = 8.0
HG_HEADS = 8
HG_DK = D_MODEL // HG_HEADS
HG_DV = D_MODEL // HG_HEADS
HG_KDIM = HG_HEADS * HG_DK
HG_VDIM = HG_HEADS * HG_DV
RET_HEADS = 4
RET_DK = D_MODEL // RET_HEADS
RET_DV = 2 * D_MODEL // RET_HEADS
RET_KDIM = RET_HEADS * RET_DK
RET_VDIM = RET_HEADS * RET_DV
ROPE_BASE = 10000.0
GLA_HEADS = 4
GLA_DK = D_MODEL // 2 // GLA_HEADS
GLA_DV = D_MODEL // GLA_HEADS
GLA_KDIM = GLA_HEADS * GLA_DK
GLA_VDIM = GLA_HEADS * GLA_DV
GLA_RANK = 16
GLA_TAU = 16.0
GATE_CHUNK = 32
N_EXPERTS = 8
FFN_EXPERT = 3584
PLE_DIM = 256

LANES = 128
SUBLANES = 8
VMEM_LIMIT = 56 << 20

ROW_TILE = 512
RG_TILE = 512
GLA_TILE = 256
RET_CHUNK = 256
MOE_TILE = 512
MOE_NF = 4


def _dot(a, b):
    return jnp.dot(a, b, preferred_element_type=F32)


def _dot_nt(a, b):
    return lax.dot_general(a, b, (((1,), (1,)), ((), ())), preferred_element_type=F32)


def _dot_tn(a, b):
    return lax.dot_general(a, b, (((0,), (0,)), ((), ())), preferred_element_type=F32)


def _sigmoid(v):
    return jax.nn.sigmoid(v)


def _silu(v):
    return v * jax.nn.sigmoid(v)


def _softplus(v):
    return jnp.maximum(v, 0.0) + jnp.log1p(jnp.exp(-jnp.abs(v)))


def _layer_norm(v, g, b):
    mu = jnp.mean(v, -1, keepdims=True)
    d = v - mu
    var = jnp.mean(d * d, -1, keepdims=True)
    return d * lax.rsqrt(var + LN_EPS) * g + b


def _params(sem):
    return pltpu.CompilerParams(dimension_semantics=sem, vmem_limit_bytes=VMEM_LIMIT)


def _full_spec(a, n_grid):
    nd = a.ndim
    if n_grid == 1:
        return pl.BlockSpec(a.shape, lambda i: (0,) * nd)
    return pl.BlockSpec(a.shape, lambda i, j: (0,) * nd)


def _row_call(body, row_ins, full_ins, outs, name, tm=ROW_TILE):
    t = row_ins[0].shape[0]
    assert t % tm == 0
    in_specs = [pl.BlockSpec((tm, a.shape[1]), lambda i: (i, 0)) for a in row_ins]
    in_specs += [_full_spec(a, 1) for a in full_ins]
    out_shape = [jax.ShapeDtypeStruct((t, c), dt) for c, dt in outs]
    out_specs = [pl.BlockSpec((tm, c), lambda i: (i, 0)) for c, dt in outs]
    return pl.pallas_call(
        body, grid=(t // tm,), in_specs=in_specs, out_specs=out_specs, out_shape=out_shape,
        compiler_params=_params(("arbitrary",)), name=name)(*row_ins, *full_ins)


def _rg_body(x_ref, win_ref, cw_ref, cb_ref, wa_ref, ba_ref, wx_ref, bx_ref, lam_ref, wout_ref,
             g_ref, b_ref, o_ref, prev_ref, hc_ref, a_ref, i_ref, h_ref):
    tc = x_ref.shape[0]
    w = RG_WIDTH

    @pl.when(pl.program_id(1) == 0)
    def _():
        prev_ref[...] = jnp.zeros_like(prev_ref)
        hc_ref[...] = jnp.zeros_like(hc_ref)

    x = x_ref[...]
    y = _dot(x.astype(BF16), win_ref[...])
    gate = jax.nn.gelu(y[:, :w], approximate=True)
    rec = y[:, w:]

    ext = jnp.concatenate([prev_ref[...], rec], axis=0)
    cw = cw_ref[...]
    u = cb_ref[...] + cw[0:1, :] * ext[SUBLANES - 3:SUBLANES - 3 + tc, :]
    for j in range(1, RG_CONV):
        off = SUBLANES - (RG_CONV - 1) + j
        u = u + cw[j:j + 1, :] * ext[off:off + tc, :]
    prev_ref[...] = rec[tc - SUBLANES:, :]

    ra, rx = [], []
    for n in range(RG_BLOCKS):
        ub = u[:, n * RG_BLOCK:(n + 1) * RG_BLOCK].astype(BF16)
        ra.append(_dot(ub, wa_ref[n]))
        rx.append(_dot(ub, wx_ref[n]))
    r = _sigmoid(jnp.concatenate(ra, axis=1) + ba_ref[...])
    ig = _sigmoid(jnp.concatenate(rx, axis=1) + bx_ref[...])
    log_a = (-RG_C * _softplus(-lam_ref[...])) * r
    th = jnp.tanh(log_a)
    a_ref[...] = jnp.exp(log_a)
    i_ref[...] = jnp.sqrt(-2.0 * th / (1.0 - th)) * (ig * u)

    row = lax.broadcasted_iota(I32, (SUBLANES, w), 0)

    def group(gi, hc):
        r0 = pl.multiple_of(gi * SUBLANES, SUBLANES)
        a8 = a_ref[pl.ds(r0, SUBLANES), :]
        b8 = i_ref[pl.ds(r0, SUBLANES), :]
        for d in (1, 2, 4):
            m = row >= d
            a_sh = pltpu.roll(a8, d, 0)
            b_sh = pltpu.roll(b8, d, 0)
            b8 = jnp.where(m, a8 * b_sh + b8, b8)
            a8 = jnp.where(m, a8 * a_sh, a8)
        h8 = a8 * hc + b8
        h_ref[pl.ds(r0, SUBLANES), :] = h8
        return h8[SUBLANES - 1:SUBLANES, :]

    hc_ref[...] = lax.fori_loop(0, tc // SUBLANES, group, hc_ref[...])

    hg = (h_ref[...] * gate).astype(BF16)
    mix = _dot(hg, wout_ref[...])
    o_ref[...] = _layer_norm(ALPHA * x + mix, g_ref[...], b_ref[...])


def _rg_layer(x2d, bsz, seq, w_in, conv_w, conv_b, w_a, b_a, w_x, b_x, lam, w_out, ln_g, ln_b):
    tc = RG_TILE
    ns = seq // tc
    row = lambda v: v.reshape(1, -1)
    fulls = [w_in.astype(BF16), conv_w, row(conv_b), w_a.astype(BF16), row(b_a), w_x.astype(BF16),
             row(b_x), row(lam), w_out.astype(BF16), row(ln_g), row(ln_b)]
    tile = pl.BlockSpec((tc, D_MODEL), lambda b, s: (b * ns + s, 0))
    return pl.pallas_call(
        _rg_body, grid=(bsz, ns),
        in_specs=[tile] + [_full_spec(a, 2) for a in fulls],
        out_specs=tile, out_shape=jax.ShapeDtypeStruct(x2d.shape, F32),
        scratch_shapes=[pltpu.VMEM((SUBLANES, RG_WIDTH), F32), pltpu.VMEM((1, RG_WIDTH), F32),
                        pltpu.VMEM((tc, RG_WIDTH), F32), pltpu.VMEM((tc, RG_WIDTH), F32),
                        pltpu.VMEM((tc, RG_WIDTH), F32)],
        compiler_params=_params(("arbitrary", "arbitrary")), name="rglru_mixer")(x2d, *fulls)


def _ple(x2, p_ref, wp_ref, wgate_ref):
    gate = _sigmoid(_dot(x2.astype(BF16), wgate_ref[...]))
    return x2 + _dot(p_ref[...].astype(BF16), wp_ref[...]) * gate


def _dense_ffn_body(x_ref, p_ref, wg_ref, wu_ref, wo_ref, g_ref, b_ref, wp_ref, wgate_ref, o_ref):
    x = x_ref[...]
    xb = x.astype(BF16)
    h = (_silu(_dot(xb, wg_ref[...])) * _dot(xb, wu_ref[...])).astype(BF16)
    x2 = _layer_norm(ALPHA * x + _dot(h, wo_ref[...]), g_ref[...], b_ref[...])
    o_ref[...] = _ple(x2, p_ref, wp_ref, wgate_ref)


def _dense_ffn_layer(x2d, p2d, w_in, w_out, ln_g, ln_b, w_proj, w_gate):
    f = w_out.shape[0]
    w_in = w_in.astype(BF16)
    fulls = [w_in[:, :f], w_in[:, f:], w_out.astype(BF16), ln_g.reshape(1, -1), ln_b.reshape(1, -1),
             w_proj.astype(BF16), w_gate.astype(BF16)]
    return _row_call(_dense_ffn_body, [x2d, p2d], fulls, [(D_MODEL, F32)], "dense_swiglu_ple", tm=256)[0]


def _out_ln_body(o_ref, x_ref, w_ref, g_ref, b_ref, y_ref):
    y_ref[...] = _layer_norm(ALPHA * x_ref[...] + _dot(o_ref[...], w_ref[...]), g_ref[...], b_ref[...])


def _out_ln(o2d, x2d, w_out, ln_g, ln_b):
    fulls = [w_out.astype(BF16), ln_g.reshape(1, -1), ln_b.reshape(1, -1)]
    return _row_call(_out_ln_body, [o2d, x2d], fulls, [(D_MODEL, F32)], "mixer_out_postnorm")[0]


def _gla_body(q_ref, k_ref, v_ref, lg_ref, gate_ref, o_ref, cum_ref, st_ref, *, heads, dk, dv):
    bsz, tb, _ = q_ref.shape
    c = GATE_CHUNK

    @pl.when(pl.program_id(0) == 0)
    def _():
        st_ref[...] = jnp.zeros_like(st_ref)

    rmod = lax.broadcasted_iota(I32, (tb, heads * dk), 0) % c
    for b in range(bsz):
        cum = lg_ref[b]
        d = 1
        while d < c:
            cum = cum + jnp.where(rmod >= d, pltpu.roll(cum, d, 0), 0.0)
            d *= 2
        cum_ref[b] = cum

    ti = lax.broadcasted_iota(I32, (c, c), 0)
    si = lax.broadcasted_iota(I32, (c, c), 1)
    causal = ti >= si

    def chunk(ci, carry):
        r0 = pl.multiple_of(ci * c, c)
        rows = pl.ds(r0, c)
        for b in range(bsz):
            for h in range(heads):
                kc = slice(h * dk, (h + 1) * dk)
                vc = slice(h * dv, (h + 1) * dv)
                cum = cum_ref[b, rows, kc]
                ref = cum[c // 2:c // 2 + 1, :]
                last = cum[c - 1:c, :]
                q = q_ref[b, rows, kc].astype(F32)
                k = k_ref[b, rows, kc].astype(F32)
                v = v_ref[b, rows, vc]
                st = st_ref[b, h]
                inter = _dot_nt((q * jnp.exp(cum)).astype(BF16), st.astype(BF16))
                sc = _dot_nt((q * jnp.exp(cum - ref)).astype(BF16),
                             (k * jnp.exp(ref - cum)).astype(BF16))
                sc = jnp.where(causal, sc, 0.0)
                o = inter + _dot(sc.astype(BF16), v)
                st_ref[b, h] = st * jnp.exp(last) + _dot_tn(v, (k * jnp.exp(last - cum)).astype(BF16))
                o = o * lax.rsqrt(jnp.mean(o * o, -1, keepdims=True) + LN_EPS)
                o_ref[b, rows, vc] = (o * gate_ref[b, rows, vc].astype(F32)).astype(BF16)
        return carry

    lax.fori_loop(0, tb // c, chunk, 0)


def _gla_core(q, k, v, lg, gate, bsz, seq, heads, dk, dv):
    tb = GLA_TILE
    r3 = lambda a: a.reshape(bsz, seq, a.shape[-1])
    ins = [r3(q), r3(k), r3(v), r3(lg), r3(gate)]
    spec = lambda a: pl.BlockSpec((bsz, tb, a.shape[-1]), lambda s: (0, s, 0))
    out = pl.pallas_call(
        functools.partial(_gla_body, heads=heads, dk=dk, dv=dv), grid=(seq // tb,),
        in_specs=[spec(a) for a in ins], out_specs=spec(ins[2]),
        out_shape=jax.ShapeDtypeStruct((bsz, seq, heads * dv), BF16),
        scratch_shapes=[pltpu.VMEM((bsz, tb, heads * dk), F32), pltpu.VMEM((bsz, heads, dv, dk), F32)],
        compiler_params=_params(("arbitrary",)), name="gated_linear_attention")(*ins)
    return out.reshape(bsz * seq, heads * dv)


def _hg_in_body(x_ref, w_ref, lb_ref, q_ref, k_ref, v_ref, lg_ref, gs_ref):
    y = _dot(x_ref[...].astype(BF16), w_ref[...])
    lb = lb_ref[...]
    f = lb + (1.0 - lb) * _sigmoid(y[:, HG_KDIM:2 * HG_KDIM])
    q_ref[...] = _silu(y[:, :HG_KDIM]).astype(BF16)
    k_ref[...] = (1.0 - f).astype(BF16)
    lg_ref[...] = jnp.log(f)
    v_ref[...] = y[:, 2 * HG_KDIM:2 * HG_KDIM + HG_VDIM].astype(BF16)
    gs_ref[...] = _silu(y[:, 2 * HG_KDIM + HG_VDIM:]).astype(BF16)


def _hgrn2_layer(x2d, bsz, seq, w_in, lb, w_out, ln_g, ln_b):
    outs = [(HG_KDIM, BF16), (HG_KDIM, BF16), (HG_VDIM, BF16), (HG_KDIM, F32), (HG_VDIM, BF16)]
    q, k, v, lg, gs = _row_call(_hg_in_body, [x2d], [w_in.astype(BF16), lb.reshape(1, -1)], outs,
                                "hgrn2_in_proj")
    o = _gla_core(q, k, v, lg, gs, bsz, seq, HG_HEADS, HG_DK, HG_DV)
    return _out_ln(o, x2d, w_out, ln_g, ln_b)


def _gla_in_body(x_ref, w_ref, wl_ref, wg_ref, bg_ref, q_ref, k_ref, v_ref, lg_ref, gs_ref):
    xb = x_ref[...].astype(BF16)
    y = _dot(xb, w_ref[...])
    gl = _dot(xb, wl_ref[...])
    z = _dot(gl.astype(BF16), wg_ref[...]) + bg_ref[...]
    lg_ref[...] = (jnp.minimum(z, 0.0) - jnp.log1p(jnp.exp(-jnp.abs(z)))) * (1.0 / GLA_TAU)
    q_ref[...] = (y[:, :GLA_KDIM] * (GLA_DK ** -0.5)).astype(BF16)
    k_ref[...] = y[:, GLA_KDIM:2 * GLA_KDIM].astype(BF16)
    v_ref[...] = y[:, 2 * GLA_KDIM:2 * GLA_KDIM + GLA_VDIM].astype(BF16)
    gs_ref[...] = _silu(y[:, 2 * GLA_KDIM + GLA_VDIM:]).astype(BF16)


def _gla_layer(x2d, bsz, seq, w_in, w_gate, b_gate, w_out, ln_g, ln_b):
    n_main = 2 * GLA_KDIM + 2 * GLA_VDIM
    w_main = w_in[:, :n_main].astype(BF16)
    w_low = jnp.pad(w_in[:, n_main:], ((0, 0), (0, LANES - GLA_RANK))).astype(BF16)
    w_gate_p = jnp.pad(w_gate, ((0, LANES - GLA_RANK), (0, 0))).astype(BF16)
    outs = [(GLA_KDIM, BF16), (GLA_KDIM, BF16), (GLA_VDIM, BF16), (GLA_KDIM, F32), (GLA_VDIM, BF16)]
    q, k, v, lg, gs = _row_call(_gla_in_body, [x2d], [w_main, w_low, w_gate_p, b_gate.reshape(1, -1)],
                                outs, "gla_in_proj")
    o = _gla_core(q, k, v, lg, gs, bsz, seq, GLA_HEADS, GLA_DK, GLA_DV)
    return _out_ln(o, x2d, w_out, ln_g, ln_b)


def _ret_in_body(x_ref, cos_ref, sin_ref, w_ref, q_ref, k_ref, v_ref, gs_ref):
    y = _dot(x_ref[...].astype(BF16), w_ref[...])
    cos = cos_ref[...]
    sin = sin_ref[...]
    half = RET_DK // 2

    def rot(base, scale, dst):
        for h in range(RET_HEADS):
            t1 = y[:, base + h * RET_DK:base + h * RET_DK + half]
            t2 = y[:, base + h * RET_DK + half:base + (h + 1) * RET_DK]
            dst[:, h * RET_DK:h * RET_DK + half] = ((t1 * cos - t2 * sin) * scale).astype(BF16)
            dst[:, h * RET_DK + half:(h + 1) * RET_DK] = ((t1 * sin + t2 * cos) * scale).astype(BF16)

    rot(0, 1.0, q_ref)
    rot(RET_KDIM, RET_DK ** -0.5, k_ref)
    v_ref[...] = y[:, 2 * RET_KDIM:2 * RET_KDIM + RET_VDIM].astype(BF16)
    gs_ref[...] = _silu(y[:, 2 * RET_KDIM + RET_VDIM:]).astype(BF16)


def _ret_body(q_ref, k_ref, v_ref, gs_ref, dq_ref, dk_ref, dm_ref, gc_ref, o_ref, st_ref):
    bsz = q_ref.shape[0]

    @pl.when(pl.program_id(0) == 0)
    def _():
        st_ref[...] = jnp.zeros_like(st_ref)

    for b in range(bsz):
        for h in range(RET_HEADS):
            kc = slice(h * RET_DK, (h + 1) * RET_DK)
            vc = slice(h * RET_DV, (h + 1) * RET_DV)
            q = q_ref[b, :, kc]
            k = k_ref[b, :, kc]
            v = v_ref[b, :, vc]
            st = st_ref[b, h]
            inter = _dot(q, st.astype(BF16)) * dq_ref[h]
            sc = _dot_nt(q, k) * dm_ref[h]
            o = inter + _dot(sc.astype(BF16), v)
            kd = (k.astype(F32) * dk_ref[h]).astype(BF16)
            st_ref[b, h] = st * gc_ref[h] + _dot_tn(kd, v)
            mu = jnp.mean(o, -1, keepdims=True)
            dlt = o - mu
            o = dlt * lax.rsqrt(jnp.mean(dlt * dlt, -1, keepdims=True) + LN_EPS)
            o_ref[b, :, vc] = (gs_ref[b, :, vc].astype(F32) * o).astype(BF16)


def _ret_layer(x2d, bsz, seq, w_in, w_out, ln_g, ln_b):
    half = RET_DK // 2
    inv = ROPE_BASE ** (-jnp.arange(0, RET_DK, 2, dtype=F32) / RET_DK)
    ang = jnp.arange(seq, dtype=F32)[:, None] * inv[None, :]
    cos, sin = jnp.cos(ang), jnp.sin(ang)
    tm = ROW_TILE
    ns = seq // tm
    t = x2d.shape[0]
    w = w_in.astype(BF16)
    outs = [(RET_KDIM, BF16), (RET_KDIM, BF16), (RET_VDIM, BF16), (RET_VDIM, BF16)]
    q, k, v, gs = pl.pallas_call(
        _ret_in_body, grid=(t // tm,),
        in_specs=[pl.BlockSpec((tm, D_MODEL), lambda i: (i, 0)),
                  pl.BlockSpec((tm, half), lambda i: (i % ns, 0)),
                  pl.BlockSpec((tm, half), lambda i: (i % ns, 0)),
                  _full_spec(w, 1)],
        out_specs=[pl.BlockSpec((tm, c), lambda i: (i, 0)) for c, _ in outs],
        out_shape=[jax.ShapeDtypeStruct((t, c), dt) for c, dt in outs],
        compiler_params=_params(("arbitrary",)), name="retention_in_proj")(x2d, cos, sin, w)

    c = RET_CHUNK
    pos = jnp.arange(c, dtype=F32)
    lgam = jnp.log1p(-jnp.exp2(-5.0 - jnp.arange(RET_HEADS, dtype=F32)))[:, None]
    decay_q = jnp.exp(lgam * (pos + 1.0))[:, :, None]
    decay_k = jnp.exp(lgam * (c - 1.0 - pos))[:, :, None]
    rel = pos[:, None] - pos[None, :]
    dmat = jnp.where(rel >= 0, jnp.exp(lgam[:, :, None] * jnp.maximum(rel, 0.0)), 0.0)
    gchunk = jnp.broadcast_to(jnp.exp(lgam * c)[:, :, None], (RET_HEADS, 1, LANES))
    decay_q = jnp.broadcast_to(decay_q, (RET_HEADS, c, LANES))
    decay_k = jnp.broadcast_to(decay_k, (RET_HEADS, c, LANES))
    r3 = lambda a: a.reshape(bsz, seq, a.shape[-1])
    ins = [r3(q), r3(k), r3(v), r3(gs)]
    tabs = [decay_q[:, :, :1], decay_k[:, :, :1], dmat, gchunk[:, :, :1]]
    spec = lambda a: pl.BlockSpec((bsz, c, a.shape[-1]), lambda s: (0, s, 0))
    o = pl.pallas_call(
        _ret_body, grid=(seq // c,),
        in_specs=[spec(a) for a in ins] + [_full_spec(a, 1) for a in tabs],
        out_specs=spec(ins[2]), out_shape=jax.ShapeDtypeStruct((bsz, seq, RET_VDIM), BF16),
        scratch_shapes=[pltpu.VMEM((bsz, RET_HEADS, RET_DK, RET_DV), F32)],
        compiler_params=_params(("arbitrary",)), name="retention")(*ins, *tabs)
    return _out_ln(o.reshape(t, RET_VDIM), x2d, w_out, ln_g, ln_b)


def _router_body(x_ref, wh_ref, wl_ref, meta_ref, cnt_ref, carry_ref):
    tm = x_ref.shape[0]

    @pl.when(pl.program_id(0) == 0)
    def _():
        carry_ref[...] = jnp.zeros_like(carry_ref)

    x = x_ref[...]
    xh = x.astype(BF16)
    xl = (x - xh.astype(F32)).astype(BF16)
    logits = _dot(xh, wh_ref[...]) + (_dot(xl, wh_ref[...]) + _dot(xh, wl_ref[...]))
    lane = lax.broadcasted_iota(I32, (tm, LANES), 1).astype(F32)
    neg = jnp.float32(-jnp.inf)
    big = jnp.float32(LANES)
    m0 = jnp.where(lane < N_EXPERTS, logits, neg)
    v1 = jnp.max(m0, -1, keepdims=True)
    i1 = jnp.min(jnp.where(m0 == v1, lane, big), -1, keepdims=True)
    m1 = jnp.where(lane == i1, neg, m0)
    v2 = jnp.max(m1, -1, keepdims=True)
    i2 = jnp.min(jnp.where(m1 == v2, lane, big), -1, keepdims=True)
    e2 = jnp.exp(v2 - v1)
    w1 = 1.0 / (1.0 + e2)
    w2 = e2 / (1.0 + e2)
    oh1 = (lane == i1).astype(F32)
    oh2 = (lane == i2).astype(F32)
    oh = oh1 + oh2
    ti = lax.broadcasted_iota(I32, (tm, tm), 0)
    si = lax.broadcasted_iota(I32, (tm, tm), 1)
    before = _dot((ti > si).astype(BF16), oh.astype(BF16)) + carry_ref[...]
    r1 = jnp.sum(before * oh1, -1, keepdims=True)
    r2 = jnp.sum(before * oh2, -1, keepdims=True)
    carry_ref[...] += jnp.sum(oh, 0, keepdims=True)
    cnt_ref[...] = jnp.broadcast_to(carry_ref[...], cnt_ref.shape)
    meta = jnp.zeros((tm, LANES), F32)
    for col, val in enumerate((i1, i2, w1, w2, r1, r2)):
        meta = jnp.where(lane == col, val, meta)
    meta_ref[...] = meta


def _moe_ffn_body(te_ref, nv_ref, src_ref, dst_ref, x_hbm, wg_ref, wu_ref, wo_ref, y_hbm,
                  xbuf, xs, acc, ybuf, gsem, ssem, *, n_tok_rows):
    tm = xs.shape[0]
    i = pl.program_id(0)
    j = pl.program_id(1)
    nf = pl.num_programs(1)
    rs = tm // MOE_NF
    nv = nv_ref[0]
    valid = i < nv
    slot = i % 2
    other = 1 - slot

    def gather_row(buf, r, tok):
        return pltpu.make_async_copy(x_hbm.at[pl.ds(tok, 1), :], xbuf.at[buf, pl.ds(r, 1), :], gsem.at[buf])

    def scatter_row(buf, r, pos):
        return pltpu.make_async_copy(ybuf.at[buf, pl.ds(r, 1), :], y_hbm.at[pl.ds(pos, 1), :], ssem.at[buf])

    def gather_tile(buf):
        return pltpu.make_async_copy(x_hbm.at[pl.ds(0, tm), :], xbuf.at[buf], gsem.at[buf])

    def scatter_tile(buf):
        return pltpu.make_async_copy(ybuf.at[buf], y_hbm.at[pl.ds(0, tm), :], ssem.at[buf])

    @pl.when((i == 0) & (j == 0))
    def _():
        def issue(r, c):
            gather_row(0, r, src_ref[r]).start()
            return c
        lax.fori_loop(0, tm, issue, 0)
        ybuf[1] = jnp.zeros(ybuf.shape[1:], ybuf.dtype)
        n_spare_tiles = (y_hbm.shape[0] - n_tok_rows) // tm
        for k in range(n_spare_tiles):
            pltpu.make_async_copy(ybuf.at[1], y_hbm.at[pl.ds(n_tok_rows + k * tm, tm), :], ssem.at[1]).start()
        for k in range(n_spare_tiles):
            scatter_tile(1).wait()

    @pl.when(valid & (j == 0))
    def _():
        gather_tile(slot).wait()
        xs[...] = xbuf[slot].astype(BF16)

    @pl.when(valid)
    def _():
        for u in range(rs):
            r = j * rs + u
            gather_row(other, r, src_ref[(i + 1) * tm + r]).start()
            scatter_row(other, r, dst_ref[i * tm + r]).start()
        xb = xs[...]
        h = (_silu(_dot(xb, wg_ref[...])) * _dot(xb, wu_ref[...])).astype(BF16)
        part = _dot(h, wo_ref[...])

        @pl.when(j == 0)
        def _():
            acc[...] = part

        @pl.when(j > 0)
        def _():
            acc[...] += part

    @pl.when(valid & (j == nf - 1))
    def _():
        @pl.when(i >= 1)
        def _():
            scatter_tile(slot).wait()
        ybuf[slot] = acc[...]

    @pl.when((i == nv) & (j == 0))
    def _():
        gather_tile(slot).wait()
        scatter_tile(slot).wait()

        def issue(r, c):
            scatter_row(other, r, dst_ref[i * tm + r]).start()
            return c
        lax.fori_loop(0, tm, issue, 0)
        scatter_tile(other).wait()


def _moe_combine_body(x_ref, y1_ref, y2_ref, meta_ref, p_ref, g_ref, b_ref, wp_ref, wgate_ref, o_ref):
    meta = meta_ref[...]
    f = meta[:, 2:3] * y1_ref[...] + meta[:, 3:4] * y2_ref[...]
    x2 = _layer_norm(ALPHA * x_ref[...] + f, g_ref[...], b_ref[...])
    o_ref[...] = _ple(x2, p_ref, wp_ref, wgate_ref)


def _moe_layer(x2d, p2d, w_router, w_in, w_out, ln_g, ln_b, w_proj, w_gate):
    t = x2d.shape[0]
    tm = MOE_TILE
    nf = MOE_NF
    tf = FFN_EXPERT // nf

    wr = jnp.pad(w_router, ((0, 0), (0, LANES - N_EXPERTS)))
    wr_hi = wr.astype(BF16)
    wr_lo = (wr - wr_hi.astype(F32)).astype(BF16)
    rt = ROW_TILE
    meta, cnt = pl.pallas_call(
        _router_body, grid=(t // rt,),
        in_specs=[pl.BlockSpec((rt, D_MODEL), lambda i: (i, 0)), _full_spec(wr_hi, 1), _full_spec(wr_lo, 1)],
        out_specs=[pl.BlockSpec((rt, LANES), lambda i: (i, 0)), pl.BlockSpec((SUBLANES, LANES), lambda i: (0, 0))],
        out_shape=[jax.ShapeDtypeStruct((t, LANES), F32), jax.ShapeDtypeStruct((SUBLANES, LANES), F32)],
        scratch_shapes=[pltpu.VMEM((1, LANES), F32)],
        compiler_params=_params(("arbitrary",)), name="moe_router")(x2d, wr_hi, wr_lo)

    n_tiles = (2 * t + N_EXPERTS * (tm - 1)) // tm + 1
    n_rows = n_tiles * tm
    counts = cnt[0, :N_EXPERTS].astype(I32)
    padded = ((counts + tm - 1) // tm) * tm
    ends = jnp.cumsum(padded)
    offs = ends - padded
    e1, e2 = meta[:, 0].astype(I32), meta[:, 1].astype(I32)
    d1 = offs[e1] + meta[:, 4].astype(I32)
    d2 = offs[e2] + meta[:, 5].astype(I32)
    slot_of_row = jnp.full((n_rows,), -1, I32).at[jnp.concatenate([d1, d2])].set(jnp.arange(2 * t, dtype=I32))
    is_pad = slot_of_row < 0
    src = jnp.where(is_pad, 0, slot_of_row % t)
    spare = 2 * t + tm + jnp.cumsum(is_pad.astype(I32)) - 1
    dst = jnp.where(is_pad, spare, slot_of_row)
    src = jnp.concatenate([src, jnp.zeros((tm,), I32)])
    dst = jnp.concatenate([2 * t + jnp.arange(tm, dtype=I32), dst])
    assert (2 * t) % tm == 0
    n_out = 2 * t + tm + (n_rows - 2 * t)
    n_valid = (ends[-1] // tm).astype(I32)
    starts = jnp.arange(n_tiles, dtype=I32) * tm
    te = jnp.minimum(jnp.sum(ends[None, :] <= starts[:, None], axis=1), N_EXPERTS - 1).astype(I32)
    te = jnp.where(jnp.arange(n_tiles) < n_valid, te, te[jnp.maximum(n_valid - 1, 0)])

    w_in = w_in.astype(BF16)
    w_out = w_out.astype(BF16)

    def fj(i, j, nv):
        return jnp.where(i < nv[0], j, nf - 1)

    y = pl.pallas_call(
        functools.partial(_moe_ffn_body, n_tok_rows=2 * t),
        grid_spec=pltpu.PrefetchScalarGridSpec(
            num_scalar_prefetch=4, grid=(n_tiles, nf),
            in_specs=[pl.BlockSpec(memory_space=pl.ANY),
                      pl.BlockSpec((None, D_MODEL, tf), lambda i, j, te, nv, s, d: (te[i], 0, fj(i, j, nv))),
                      pl.BlockSpec((None, D_MODEL, tf), lambda i, j, te, nv, s, d: (te[i], 0, nf + fj(i, j, nv))),
                      pl.BlockSpec((None, tf, D_MODEL), lambda i, j, te, nv, s, d: (te[i], fj(i, j, nv), 0))],
            out_specs=pl.BlockSpec(memory_space=pl.ANY),
            scratch_shapes=[pltpu.VMEM((2, tm, D_MODEL), F32), pltpu.VMEM((tm, D_MODEL), BF16),
                            pltpu.VMEM((tm, D_MODEL), F32), pltpu.VMEM((2, tm, D_MODEL), F32),
                            pltpu.SemaphoreType.DMA((2,)), pltpu.SemaphoreType.DMA((2,))]),
        out_shape=jax.ShapeDtypeStruct((n_out, D_MODEL), F32),
        compiler_params=_params(("arbitrary", "arbitrary")), name="moe_expert_ffn",
    )(te, n_valid.reshape(1), src, dst, x2d, w_in, w_in, w_out)

    ct = 256
    nct = t // ct
    fulls = [ln_g.reshape(1, -1), ln_b.reshape(1, -1), w_proj.astype(BF16), w_gate.astype(BF16)]
    row = lambda c: pl.BlockSpec((ct, c), lambda i: (i, 0))
    return pl.pallas_call(
        _moe_combine_body, grid=(nct,),
        in_specs=[row(D_MODEL), row(D_MODEL), pl.BlockSpec((ct, D_MODEL), lambda i: (nct + i, 0)),
                  row(LANES), row(PLE_DIM)] + [_full_spec(a, 1) for a in fulls],
        out_specs=row(D_MODEL), out_shape=jax.ShapeDtypeStruct((t, D_MODEL), F32),
        compiler_params=_params(("arbitrary",)), name="moe_combine_postnorm_ple",
    )(x2d, y, y, meta, p2d, *fulls)


def kernel(x, p, rg_w_in, rg_conv_w, rg_conv_b, rg_w_a, rg_b_a, rg_w_x, rg_b_x, rg_lambda, rg_w_out,
           hg_w_in, hg_lb_logits, hg_w_out, ret_w_in, ret_w_out, gla_w_in, gla_w_gate, gla_b_gate,
           gla_w_out, dense_w_in, dense_w_out, moe_w_router, moe_w_in, moe_w_out, ple_w_proj,
           ple_w_gate, ln_mix_g, ln_mix_b, ln_ffn_g, ln_ffn_b):
    bsz, seq, d = x.shape
    t = bsz * seq
    depth = p.shape[0]
    lb_sm = jax.nn.softmax(hg_lb_logits.astype(F32), axis=0)
    lower_bounds = jnp.cumsum(lb_sm, axis=0) - lb_sm[0:1]
    h = x.reshape(t, d)
    for i in range(depth):
        kind, j = i % 4, i // 4
        if kind == 0:
            h = _rg_layer(h, bsz, seq, rg_w_in[j], rg_conv_w[j], rg_conv_b[j], rg_w_a[j], rg_b_a[j],
                          rg_w_x[j], rg_b_x[j], rg_lambda[j], rg_w_out[j], ln_mix_g[i], ln_mix_b[i])
        elif kind == 1:
            h = _hgrn2_layer(h, bsz, seq, hg_w_in[j], lower_bounds[i], hg_w_out[j], ln_mix_g[i], ln_mix_b[i])
        elif kind == 2:
            h = _ret_layer(h, bsz, seq, ret_w_in[j], ret_w_out[j], ln_mix_g[i], ln_mix_b[i])
        else:
            h = _gla_layer(h, bsz, seq, gla_w_in[j], gla_w_gate[j], gla_b_gate[j], gla_w_out[j],
                           ln_mix_g[i], ln_mix_b[i])
        p2d = p[i].reshape(t, p.shape[-1])
        if i % 2 == 0:
            h = _dense_ffn_layer(h, p2d, dense_w_in[i // 2], dense_w_out[i // 2], ln_ffn_g[i], ln_ffn_b[i],
                                 ple_w_proj[i], ple_w_gate[i])
        else:
            h = _moe_layer(h, p2d, moe_w_router[i // 2], moe_w_in[i // 2], moe_w_out[i // 2],
                           ln_ffn_g[i], ln_ffn_b[i], ple_w_proj[i], ple_w_gate[i])
    return h.reshape(bsz, seq, d)
```

```python
import functools
import math

import jax
import jax.numpy as jnp
from jax import lax
from jax.experimental import pallas as pl
from jax.experimental.pallas import tpu as pltpu

F32 = jnp.float32
BF16 = jnp.bfloat16
I32 = jnp.int32

D_MODEL = 1024
DEPTH = 4
ALPHA = (2.0 * DEPTH) ** 0.25
LN_EPS = 1e-5
RG_WIDTH = D_MODEL
RG_BLOCK = 256
RG_BLOCKS = RG_WIDTH // RG_BLOCK
RG_CONV = 4
RG_C = 8.0
HG_HEADS = 8
HG_DK = D_MODEL // HG_HEADS
HG_DV = D_MODEL // HG_HEADS
HG_KDIM = HG_HEADS * HG_DK
HG_VDIM = HG_HEADS * HG_DV
RET_HEADS = 4
RET_DK = D_MODEL // RET_HEADS
RET_DV = 2 * D_MODEL // RET_HEADS
RET_KDIM = RET_HEADS * RET_DK
RET_VDIM = RET_HEADS * RET_DV
ROPE_BASE = 10000.0
GLA_HEADS = 4
GLA_DK = D_MODEL // 2 // GLA_HEADS
GLA_DV = D_MODEL // GLA_HEADS
GLA_KDIM = GLA_HEADS * GLA_DK
GLA_VDIM = GLA_HEADS * GLA_DV
GLA_RANK = 16
GLA_TAU = 16.0
GATE_CHUNK = 32
N_EXPERTS = 8
FFN_EXPERT = 3584
PLE_DIM = 256

LANES = 128
SUBLANES = 8
VMEM_LIMIT = 56 << 20

ROW_TILE = 512
RG_TILE = 512
GLA_TILE = 256
RET_CHUNK = 256
MOE_TILE = 512
MOE_NF = 2


def _dot(a, b):
    return jnp.dot(a, b, preferred_element_type=F32)


def _dot_nt(a, b):
    return lax.dot_general(a, b, (((1,), (1,)), ((), ())), preferred_element_type=F32)


def _dot_tn(a, b):
    return lax.dot_general(a, b, (((0,), (0,)), ((), ())), preferred_element_type=F32)


def _sigmoid(v):
    return jax.nn.sigmoid(v)


def _silu(v):
    return v * jax.nn.sigmoid(v)


def _softplus(v):
    return jnp.maximum(v, 0.0) + jnp.log1p(jnp.exp(-jnp.abs(v)))


def _layer_norm(v, g, b):
    mu = jnp.mean(v, -1, keepdims=True)
    d = v - mu
    var = jnp.mean(d * d, -1, keepdims=True)
    return d * lax.rsqrt(var + LN_EPS) * g + b


def _params(sem):
    return pltpu.CompilerParams(dimension_semantics=sem, vmem_limit_bytes=VMEM_LIMIT)


def _full_spec(a, n_grid):
    nd = a.ndim
    if n_grid == 1:
        return pl.BlockSpec(a.shape, lambda i: (0,) * nd)
    return pl.BlockSpec(a.shape, lambda i, j: (0,) * nd)


def _row_call(body, row_ins, full_ins, outs, name, tm=ROW_TILE):
    t = row_ins[0].shape[0]
    assert t % tm == 0
    in_specs = [pl.BlockSpec((tm, a.shape[1]), lambda i: (i, 0)) for a in row_ins]
    in_specs += [_full_spec(a, 1) for a in full_ins]
    out_shape = [jax.ShapeDtypeStruct((t, c), dt) for c, dt in outs]
    out_specs = [pl.BlockSpec((tm, c), lambda i: (i, 0)) for c, dt in outs]
    return pl.pallas_call(
        body, grid=(t // tm,), in_specs=in_specs, out_specs=out_specs, out_shape=out_shape,
        compiler_params=_params(("arbitrary",)), name=name)(*row_ins, *full_ins)


def _rg_body(x_ref, win_ref, cw_ref, cb_ref, wa_ref, ba_ref, wx_ref, bx_ref, lam_ref, wout_ref,
             g_ref, b_ref, o_ref, prev_ref, hc_ref, a_ref, i_ref, h_ref):
    tc = x_ref.shape[0]
    w = RG_WIDTH

    @pl.when(pl.program_id(1) == 0)
    def _():
        prev_ref[...] = jnp.zeros_like(prev_ref)
        hc_ref[...] = jnp.zeros_like(hc_ref)

    x = x_ref[...]
    y = _dot(x.astype(BF16), win_ref[...])
    gate = jax.nn.gelu(y[:, :w], approximate=True)
    rec = y[:, w:]

    ext = jnp.concatenate([prev_ref[...], rec], axis=0)
    cw = cw_ref[...]
    u = cb_ref[...] + cw[0:1, :] * ext[SUBLANES - 3:SUBLANES - 3 + tc, :]
    for j in range(1, RG_CONV):
        off = SUBLANES - (RG_CONV - 1) + j
        u = u + cw[j:j + 1, :] * ext[off:off + tc, :]
    prev_ref[...] = rec[tc - SUBLANES:, :]

    ra, rx = [], []
    for n in range(RG_BLOCKS):
        ub = u[:, n * RG_BLOCK:(n + 1) * RG_BLOCK].astype(BF16)
        ra.append(_dot(ub, wa_ref[n]))
        rx.append(_dot(ub, wx_ref[n]))
    r = _sigmoid(jnp.concatenate(ra, axis=1) + ba_ref[...])
    ig = _sigmoid(jnp.concatenate(rx, axis=1) + bx_ref[...])
    log_a = (-RG_C * _softplus(-lam_ref[...])) * r
    th = jnp.tanh(log_a)
    a_ref[...] = jnp.exp(log_a)
    i_ref[...] = jnp.sqrt(-2.0 * th / (1.0 - th)) * (ig * u)

    row = lax.broadcasted_iota(I32, (SUBLANES, w), 0)

    def group(gi, hc):
        r0 = pl.multiple_of(gi * SUBLANES, SUBLANES)
        a8 = a_ref[pl.ds(r0, SUBLANES), :]
        b8 = i_ref[pl.ds(r0, SUBLANES), :]
        for d in (1, 2, 4):
            m = row >= d
            a_sh = pltpu.roll(a8, d, 0)
            b_sh = pltpu.roll(b8, d, 0)
            b8 = jnp.where(m, a8 * b_sh + b8, b8)
            a8 = jnp.where(m, a8 * a_sh, a8)
        h8 = a8 * hc + b8
        h_ref[pl.ds(r0, SUBLANES), :] = h8
        return h8[SUBLANES - 1:SUBLANES, :]

    hc_ref[...] = lax.fori_loop(0, tc // SUBLANES, group, hc_ref[...])

    hg = (h_ref[...] * gate).astype(BF16)
    mix = _dot(hg, wout_ref[...])
    o_ref[...] = _layer_norm(ALPHA * x + mix, g_ref[...], b_ref[...])


def _rg_layer(x2d, bsz, seq, w_in, conv_w, conv_b, w_a, b_a, w_x, b_x, lam, w_out, ln_g, ln_b):
    tc = RG_TILE
    ns = seq // tc
    row = lambda v: v.reshape(1, -1)
    fulls = [w_in.astype(BF16), conv_w, row(conv_b), w_a.astype(BF16), row(b_a), w_x.astype(BF16),
             row(b_x), row(lam), w_out.astype(BF16), row(ln_g), row(ln_b)]
    tile = pl.BlockSpec((tc, D_MODEL), lambda b, s: (b * ns + s, 0))
    return pl.pallas_call(
        _rg_body, grid=(bsz, ns),
        in_specs=[tile] + [_full_spec(a, 2) for a in fulls],
        out_specs=tile, out_shape=jax.ShapeDtypeStruct(x2d.shape, F32),
        scratch_shapes=[pltpu.VMEM((SUBLANES, RG_WIDTH), F32), pltpu.VMEM((1, RG_WIDTH), F32),
                        pltpu.VMEM((tc, RG_WIDTH), F32), pltpu.VMEM((tc, RG_WIDTH), F32),
                        pltpu.VMEM((tc, RG_WIDTH), F32)],
        compiler_params=_params(("arbitrary", "arbitrary")), name="rglru_mixer")(x2d, *fulls)


def _ple(x2, p_ref, wp_ref, wgate_ref):
    gate = _sigmoid(_dot(x2.astype(BF16), wgate_ref[...]))
    return x2 + _dot(p_ref[...].astype(BF16), wp_ref[...]) * gate


def _dense_ffn_body(x_ref, p_ref, wg_ref, wu_ref, wo_ref, g_ref, b_ref, wp_ref, wgate_ref, o_ref):
    x = x_ref[...]
    xb = x.astype(BF16)
    h = (_silu(_dot(xb, wg_ref[...])) * _dot(xb, wu_ref[...])).astype(BF16)
    x2 = _layer_norm(ALPHA * x + _dot(h, wo_ref[...]), g_ref[...], b_ref[...])
    o_ref[...] = _ple(x2, p_ref, wp_ref, wgate_ref)


def _dense_ffn_layer(x2d, p2d, w_in, w_out, ln_g, ln_b, w_proj, w_gate):
    f = w_out.shape[0]
    w_in = w_in.astype(BF16)
    fulls = [w_in[:, :f], w_in[:, f:], w_out.astype(BF16), ln_g.reshape(1, -1), ln_b.reshape(1, -1),
             w_proj.astype(BF16), w_gate.astype(BF16)]
    return _row_call(_dense_ffn_body, [x2d, p2d], fulls, [(D_MODEL, F32)], "dense_swiglu_ple", tm=256)[0]


def _out_ln_body(o_ref, x_ref, w_ref, g_ref, b_ref, y_ref):
    y_ref[...] = _layer_norm(ALPHA * x_ref[...] + _dot(o_ref[...], w_ref[...]), g_ref[...], b_ref[...])


def _out_ln(o2d, x2d, w_out, ln_g, ln_b):
    fulls = [w_out.astype(BF16), ln_g.reshape(1, -1), ln_b.reshape(1, -1)]
    return _row_call(_out_ln_body, [o2d, x2d], fulls, [(D_MODEL, F32)], "mixer_out_postnorm")[0]


def _gla_body(q_ref, k_ref, v_ref, lg_ref, gate_ref, o_ref, cum_ref, st_ref, *, heads, dk, dv):
    bsz, tb, _ = q_ref.shape
    c = GATE_CHUNK

    @pl.when(pl.program_id(0) == 0)
    def _():
        st_ref[...] = jnp.zeros_like(st_ref)

    rmod = lax.broadcasted_iota(I32, (tb, heads * dk), 0) % c
    for b in range(bsz):
        cum = lg_ref[b]
        d = 1
        while d < c:
            cum = cum + jnp.where(rmod >= d, pltpu.roll(cum, d, 0), 0.0)
            d *= 2
        cum_ref[b] = cum

    ti = lax.broadcasted_iota(I32, (c, c), 0)
    si = lax.broadcasted_iota(I32, (c, c), 1)
    causal = ti >= si

    def chunk(ci, carry):
        r0 = pl.multiple_of(ci * c, c)
        rows = pl.ds(r0, c)
        for b in range(bsz):
            for h in range(heads):
                kc = slice(h * dk, (h + 1) * dk)
                vc = slice(h * dv, (h + 1) * dv)
                cum = cum_ref[b, rows, kc]
                ref = cum[c // 2:c // 2 + 1, :]
                last = cum[c - 1:c, :]
                q = q_ref[b, rows, kc].astype(F32)
                k = k_ref[b, rows, kc].astype(F32)
                v = v_ref[b, rows, vc]
                st = st_ref[b, h]
                inter = _dot_nt((q * jnp.exp(cum)).astype(BF16), st.astype(BF16))
                sc = _dot_nt((q * jnp.exp(cum - ref)).astype(BF16),
                             (k * jnp.exp(ref - cum)).astype(BF16))
                sc = jnp.where(causal, sc, 0.0)
                o = inter + _dot(sc.astype(BF16), v)
                st_ref[b, h] = st * jnp.exp(last) + _dot_tn(v, (k * jnp.exp(last - cum)).astype(BF16))
                o = o * lax.rsqrt(jnp.mean(o * o, -1, keepdims=True) + LN_EPS)
                o_ref[b, rows, vc] = (o * gate_ref[b, rows, vc].astype(F32)).astype(BF16)
        return carry

    lax.fori_loop(0, tb // c, chunk, 0)


def _gla_core(q, k, v, lg, gate, bsz, seq, heads, dk, dv):
    tb = GLA_TILE
    r3 = lambda a: a.reshape(bsz, seq, a.shape[-1])
    ins = [r3(q), r3(k), r3(v), r3(lg), r3(gate)]
    spec = lambda a: pl.BlockSpec((bsz, tb, a.shape[-1]), lambda s: (0, s, 0))
    out = pl.pallas_call(
        functools.partial(_gla_body, heads=heads, dk=dk, dv=dv), grid=(seq // tb,),
        in_specs=[spec(a) for a in ins], out_specs=spec(ins[2]),
        out_shape=jax.ShapeDtypeStruct((bsz, seq, heads * dv), BF16),
        scratch_shapes=[pltpu.VMEM((bsz, tb, heads * dk), F32), pltpu.VMEM((bsz, heads, dv, dk), F32)],
        compiler_params=_params(("arbitrary",)), name="gated_linear_attention")(*ins)
    return out.reshape(bsz * seq, heads * dv)


def _hg_in_body(x_ref, w_ref, lb_ref, q_ref, k_ref, v_ref, lg_ref, gs_ref):
    y = _dot(x_ref[...].astype(BF16), w_ref[...])
    lb = lb_ref[...]
    f = lb + (1.0 - lb) * _sigmoid(y[:, HG_KDIM:2 * HG_KDIM])
    q_ref[...] = _silu(y[:, :HG_KDIM]).astype(BF16)
    k_ref[...] = (1.0 - f).astype(BF16)
    lg_ref[...] = jnp.log(f)
    v_ref[...] = y[:, 2 * HG_KDIM:2 * HG_KDIM + HG_VDIM].astype(BF16)
    gs_ref[...] = _silu(y[:, 2 * HG_KDIM + HG_VDIM:]).astype(BF16)


def _hgrn2_layer(x2d, bsz, seq, w_in, lb, w_out, ln_g, ln_b):
    outs = [(HG_KDIM, BF16), (HG_KDIM, BF16), (HG_VDIM, BF16), (HG_KDIM, F32), (HG_VDIM, BF16)]
    q, k, v, lg, gs = _row_call(_hg_in_body, [x2d], [w_in.astype(BF16), lb.reshape(1, -1)], outs,
                                "hgrn2_in_proj")
    o = _gla_core(q, k, v, lg, gs, bsz, seq, HG_HEADS, HG_DK, HG_DV)
    return _out_ln(o, x2d, w_out, ln_g, ln_b)


def _gla_in_body(x_ref, w_ref, wl_ref, wg_ref, bg_ref, q_ref, k_ref, v_ref, lg_ref, gs_ref):
    xb = x_ref[...].astype(BF16)
    y = _dot(xb, w_ref[...])
    gl = _dot(xb, wl_ref[...])
    z = _dot(gl.astype(BF16), wg_ref[...]) + bg_ref[...]
    lg_ref[...] = (jnp.minimum(z, 0.0) - jnp.log1p(jnp.exp(-jnp.abs(z)))) * (1.0 / GLA_TAU)
    q_ref[...] = (y[:, :GLA_KDIM] * (GLA_DK ** -0.5)).astype(BF16)
    k_ref[...] = y[:, GLA_KDIM:2 * GLA_KDIM].astype(BF16)
    v_ref[...] = y[:, 2 * GLA_KDIM:2 * GLA_KDIM + GLA_VDIM].astype(BF16)
    gs_ref[...] = _silu(y[:, 2 * GLA_KDIM + GLA_VDIM:]).astype(BF16)


def _gla_layer(x2d, bsz, seq, w_in, w_gate, b_gate, w_out, ln_g, ln_b):
    n_main = 2 * GLA_KDIM + 2 * GLA_VDIM
    w_main = w_in[:, :n_main].astype(BF16)
    w_low = jnp.pad(w_in[:, n_main:], ((0, 0), (0, LANES - GLA_RANK))).astype(BF16)
    w_gate_p = jnp.pad(w_gate, ((0, LANES - GLA_RANK), (0, 0))).astype(BF16)
    outs = [(GLA_KDIM, BF16), (GLA_KDIM, BF16), (GLA_VDIM, BF16), (GLA_KDIM, F32), (GLA_VDIM, BF16)]
    q, k, v, lg, gs = _row_call(_gla_in_body, [x2d], [w_main, w_low, w_gate_p, b_gate.reshape(1, -1)],
                                outs, "gla_in_proj")
    o = _gla_core(q, k, v, lg, gs, bsz, seq, GLA_HEADS, GLA_DK, GLA_DV)
    return _out_ln(o, x2d, w_out, ln_g, ln_b)


def _ret_in_body(x_ref, cos_ref, sin_ref, w_ref, q_ref, k_ref, v_ref, gs_ref):
    y = _dot(x_ref[...].astype(BF16), w_ref[...])
    cos = cos_ref[...]
    sin = sin_ref[...]
    half = RET_DK // 2

    def rot(base, scale, dst):
        for h in range(RET_HEADS):
            t1 = y[:, base + h * RET_DK:base + h * RET_DK + half]
            t2 = y[:, base + h * RET_DK + half:base + (h + 1) * RET_DK]
            dst[:, h * RET_DK:h * RET_DK + half] = ((t1 * cos - t2 * sin) * scale).astype(BF16)
            dst[:, h * RET_DK + half:(h + 1) * RET_DK] = ((t1 * sin + t2 * cos) * scale).astype(BF16)

    rot(0, 1.0, q_ref)
    rot(RET_KDIM, RET_DK ** -0.5, k_ref)
    v_ref[...] = y[:, 2 * RET_KDIM:2 * RET_KDIM + RET_VDIM].astype(BF16)
    gs_ref[...] = _silu(y[:, 2 * RET_KDIM + RET_VDIM:]).astype(BF16)


def _ret_body(q_ref, k_ref, v_ref, gs_ref, dq_ref, dk_ref, dm_ref, gc_ref, o_ref, st_ref):
    bsz = q_ref.shape[0]

    @pl.when(pl.program_id(0) == 0)
    def _():
        st_ref[...] = jnp.zeros_like(st_ref)

    for b in range(bsz):
        for h in range(RET_HEADS):
            kc = slice(h * RET_DK, (h + 1) * RET_DK)
            vc = slice(h * RET_DV, (h + 1) * RET_DV)
            q = q_ref[b, :, kc]
            k = k_ref[b, :, kc]
            v = v_ref[b, :, vc]
            st = st_ref[b, h]
            inter = _dot(q, st.astype(BF16)) * dq_ref[h]
            sc = _dot_nt(q, k) * dm_ref[h]
            o = inter + _dot(sc.astype(BF16), v)
            kd = (k.astype(F32) * dk_ref[h]).astype(BF16)
            st_ref[b, h] = st * gc_ref[h] + _dot_tn(kd, v)
            mu = jnp.mean(o, -1, keepdims=True)
            dlt = o - mu
            o = dlt * lax.rsqrt(jnp.mean(dlt * dlt, -1, keepdims=True) + LN_EPS)
            o_ref[b, :, vc] = (gs_ref[b, :, vc].astype(F32) * o).astype(BF16)


def _ret_layer(x2d, bsz, seq, w_in, w_out, ln_g, ln_b):
    half = RET_DK // 2
    inv = ROPE_BASE ** (-jnp.arange(0, RET_DK, 2, dtype=F32) / RET_DK)
    ang = jnp.arange(seq, dtype=F32)[:, None] * inv[None, :]
    cos, sin = jnp.cos(ang), jnp.sin(ang)
    tm = ROW_TILE
    ns = seq // tm
    t = x2d.shape[0]
    w = w_in.astype(BF16)
    outs = [(RET_KDIM, BF16), (RET_KDIM, BF16), (RET_VDIM, BF16), (RET_VDIM, BF16)]
    q, k, v, gs = pl.pallas_call(
        _ret_in_body, grid=(t // tm,),
        in_specs=[pl.BlockSpec((tm, D_MODEL), lambda i: (i, 0)),
                  pl.BlockSpec((tm, half), lambda i: (i % ns, 0)),
                  pl.BlockSpec((tm, half), lambda i: (i % ns, 0)),
                  _full_spec(w, 1)],
        out_specs=[pl.BlockSpec((tm, c), lambda i: (i, 0)) for c, _ in outs],
        out_shape=[jax.ShapeDtypeStruct((t, c), dt) for c, dt in outs],
        compiler_params=_params(("arbitrary",)), name="retention_in_proj")(x2d, cos, sin, w)

    c = RET_CHUNK
    pos = jnp.arange(c, dtype=F32)
    lgam = jnp.log1p(-jnp.exp2(-5.0 - jnp.arange(RET_HEADS, dtype=F32)))[:, None]
    decay_q = jnp.exp(lgam * (pos + 1.0))[:, :, None]
    decay_k = jnp.exp(lgam * (c - 1.0 - pos))[:, :, None]
    rel = pos[:, None] - pos[None, :]
    dmat = jnp.where(rel >= 0, jnp.exp(lgam[:, :, None] * jnp.maximum(rel, 0.0)), 0.0)
    gchunk = jnp.broadcast_to(jnp.exp(lgam * c)[:, :, None], (RET_HEADS, 1, LANES))
    decay_q = jnp.broadcast_to(decay_q, (RET_HEADS, c, LANES))
    decay_k = jnp.broadcast_to(decay_k, (RET_HEADS, c, LANES))
    r3 = lambda a: a.reshape(bsz, seq, a.shape[-1])
    ins = [r3(q), r3(k), r3(v), r3(gs)]
    tabs = [decay_q[:, :, :1], decay_k[:, :, :1], dmat, gchunk[:, :, :1]]
    spec = lambda a: pl.BlockSpec((bsz, c, a.shape[-1]), lambda s: (0, s, 0))
    o = pl.pallas_call(
        _ret_body, grid=(seq // c,),
        in_specs=[spec(a) for a in ins] + [_full_spec(a, 1) for a in tabs],
        out_specs=spec(ins[2]), out_shape=jax.ShapeDtypeStruct((bsz, seq, RET_VDIM), BF16),
        scratch_shapes=[pltpu.VMEM((bsz, RET_HEADS, RET_DK, RET_DV), F32)],
        compiler_params=_params(("arbitrary",)), name="retention")(*ins, *tabs)
    return _out_ln(o.reshape(t, RET_VDIM), x2d, w_out, ln_g, ln_b)


def _router_body(x_ref, wh_ref, wl_ref, meta_ref, cnt_ref, carry_ref):
    tm = x_ref.shape[0]

    @pl.when(pl.program_id(0) == 0)
    def _():
        carry_ref[...] = jnp.zeros_like(carry_ref)

    x = x_ref[...]
    xh = x.astype(BF16)
    xl = (x - xh.astype(F32)).astype(BF16)
    logits = _dot(xh, wh_ref[...]) + (_dot(xl, wh_ref[...]) + _dot(xh, wl_ref[...]))
    lane = lax.broadcasted_iota(I32, (tm, LANES), 1).astype(F32)
    neg = jnp.float32(-jnp.inf)
    big = jnp.float32(LANES)
    m0 = jnp.where(lane < N_EXPERTS, logits, neg)
    v1 = jnp.max(m0, -1, keepdims=True)
    i1 = jnp.min(jnp.where(m0 == v1, lane, big), -1, keepdims=True)
    m1 = jnp.where(lane == i1, neg, m0)
    v2 = jnp.max(m1, -1, keepdims=True)
    i2 = jnp.min(jnp.where(m1 == v2, lane, big), -1, keepdims=True)
    e2 = jnp.exp(v2 - v1)
    w1 = 1.0 / (1.0 + e2)
    w2 = e2 / (1.0 + e2)
    oh1 = (lane == i1).astype(F32)
    oh2 = (lane == i2).astype(F32)
    oh = oh1 + oh2
    ti = lax.broadcasted_iota(I32, (tm, tm), 0)
    si = lax.broadcasted_iota(I32, (tm, tm), 1)
    before = _dot((ti > si).astype(BF16), oh.astype(BF16)) + carry_ref[...]
    r1 = jnp.sum(before * oh1, -1, keepdims=True)
    r2 = jnp.sum(before * oh2, -1, keepdims=True)
    carry_ref[...] += jnp.sum(oh, 0, keepdims=True)
    cnt_ref[...] = jnp.broadcast_to(carry_ref[...], cnt_ref.shape)
    meta = jnp.zeros((tm, LANES), F32)
    for col, val in enumerate((i1, i2, w1, w2, r1, r2)):
        meta = jnp.where(lane == col, val, meta)
    meta_ref[...] = meta


def _moe_ffn_body(te_ref, nv_ref, src_ref, dst_ref, x_hbm, wg_ref, wu_ref, wo_ref, y_hbm,
                  xbuf, xs, acc, ybuf, gsem, ssem, *, n_tok_rows):
    tm = xs.shape[0]
    i = pl.program_id(0)
    j = pl.program_id(1)
    nf = pl.num_programs(1)
    rs = tm // MOE_NF
    nv = nv_ref[0]
    valid = i < nv
    slot = i % 2
    other = 1 - slot

    def gather_row(buf, r, tok):
        return pltpu.make_async_copy(x_hbm.at[pl.ds(tok, 1), :], xbuf.at[buf, pl.ds(r, 1), :], gsem.at[buf])

    def scatter_row(buf, r, pos):
        return pltpu.make_async_copy(ybuf.at[buf, pl.ds(r, 1), :], y_hbm.at[pl.ds(pos, 1), :], ssem.at[buf])

    def gather_tile(buf):
        return pltpu.make_async_copy(x_hbm.at[pl.ds(0, tm), :], xbuf.at[buf], gsem.at[buf])

    def scatter_tile(buf):
        return pltpu.make_async_copy(ybuf.at[buf], y_hbm.at[pl.ds(0, tm), :], ssem.at[buf])

    @pl.when((i == 0) & (j == 0))
    def _():
        def issue(r, c):
            gather_row(0, r, src_ref[r]).start()
            return c
        lax.fori_loop(0, tm, issue, 0)
        ybuf[1] = jnp.zeros(ybuf.shape[1:], ybuf.dtype)
        n_spare_tiles = (y_hbm.shape[0] - n_tok_rows) // tm
        for k in range(n_spare_tiles):
            pltpu.make_async_copy(ybuf.at[1], y_hbm.at[pl.ds(n_tok_rows + k * tm, tm), :], ssem.at[1]).start()
        for k in range(n_spare_tiles):
            scatter_tile(1).wait()

    @pl.when(valid & (j == 0))
    def _():
        gather_tile(slot).wait()
        xs[...] = xbuf[slot].astype(BF16)

    @pl.when(valid)
    def _():
        for u in range(rs):
            r = j * rs + u
            gather_row(other, r, src_ref[(i + 1) * tm + r]).start()
            scatter_row(other, r, dst_ref[i * tm + r]).start()
        xb = xs[...]
        h = (_silu(_dot(xb, wg_ref[...])) * _dot(xb, wu_ref[...])).astype(BF16)
        part = _dot(h, wo_ref[...])

        @pl.when(j == 0)
        def _():
            acc[...] = part

        @pl.when(j > 0)
        def _():
            acc[...] += part

    @pl.when(valid & (j == nf - 1))
    def _():
        @pl.when(i >= 1)
        def _():
            scatter_tile(slot).wait()
        ybuf[slot] = acc[...]

    @pl.when((i == nv) & (j == 0))
    def _():
        gather_tile(slot).wait()
        scatter_tile(slot).wait()

        def issue(r, c):
            scatter_row(other, r, dst_ref[i * tm + r]).start()
            return c
        lax.fori_loop(0, tm, issue, 0)
        scatter_tile(other).wait()


def _moe_combine_body(x_ref, y1_ref, y2_ref, meta_ref, p_ref, g_ref, b_ref, wp_ref, wgate_ref, o_ref):
    meta = meta_ref[...]
    f = meta[:, 2:3] * y1_ref[...] + meta[:, 3:4] * y2_ref[...]
    x2 = _layer_norm(ALPHA * x_ref[...] + f, g_ref[...], b_ref[...])
    o_ref[...] = _ple(x2, p_ref, wp_ref, wgate_ref)


def _moe_layer(x2d, p2d, w_router, w_in, w_out, layer, ln_g, ln_b, w_proj, w_gate):
    t = x2d.shape[0]
    tm = MOE_TILE
    nf = MOE_NF
    tf = FFN_EXPERT // nf

    wr = jnp.pad(w_router, ((0, 0), (0, LANES - N_EXPERTS)))
    wr_hi = wr.astype(BF16)
    wr_lo = (wr - wr_hi.astype(F32)).astype(BF16)
    rt = ROW_TILE
    meta, cnt = pl.pallas_call(
        _router_body, grid=(t // rt,),
        in_specs=[pl.BlockSpec((rt, D_MODEL), lambda i: (i, 0)), _full_spec(wr_hi, 1), _full_spec(wr_lo, 1)],
        out_specs=[pl.BlockSpec((rt, LANES), lambda i: (i, 0)), pl.BlockSpec((SUBLANES, LANES), lambda i: (0, 0))],
        out_shape=[jax.ShapeDtypeStruct((t, LANES), F32), jax.ShapeDtypeStruct((SUBLANES, LANES), F32)],
        scratch_shapes=[pltpu.VMEM((1, LANES), F32)],
        compiler_params=_params(("arbitrary",)), name="moe_router")(x2d, wr_hi, wr_lo)

    n_tiles = (2 * t + N_EXPERTS * (tm - 1)) // tm + 1
    n_rows = n_tiles * tm
    counts = cnt[0, :N_EXPERTS].astype(I32)
    padded = ((counts + tm - 1) // tm) * tm
    ends = jnp.cumsum(padded)
    offs = ends - padded
    e1, e2 = meta[:, 0].astype(I32), meta[:, 1].astype(I32)
    d1 = offs[e1] + meta[:, 4].astype(I32)
    d2 = offs[e2] + meta[:, 5].astype(I32)
    slot_of_row = jnp.full((n_rows,), -1, I32).at[jnp.concatenate([d1, d2])].set(
        jnp.arange(2 * t, dtype=I32), unique_indices=True, mode="promise_in_bounds")
    is_pad = slot_of_row < 0
    src = jnp.where(is_pad, 0, slot_of_row % t)
    spare = 2 * t + tm + jnp.cumsum(is_pad.astype(I32)) - 1
    dst = jnp.where(is_pad, spare, slot_of_row)
    src = jnp.concatenate([src, jnp.zeros((tm,), I32)])
    dst = jnp.concatenate([2 * t + jnp.arange(tm, dtype=I32), dst])
    assert (2 * t) % tm == 0
    n_out = 2 * t + tm + (n_rows - 2 * t)
    n_valid = (ends[-1] // tm).astype(I32)
    starts = jnp.arange(n_tiles, dtype=I32) * tm
    te = jnp.minimum(jnp.sum(ends[None, :] <= starts[:, None], axis=1), N_EXPERTS - 1).astype(I32)
    te = jnp.where(jnp.arange(n_tiles) < n_valid, te, te[jnp.maximum(n_valid - 1, 0)])

    def fj(i, j, nv):
        return jnp.where(i < nv[0], j, nf - 1)

    lyr = layer

    y = pl.pallas_call(
        functools.partial(_moe_ffn_body, n_tok_rows=2 * t),
        grid_spec=pltpu.PrefetchScalarGridSpec(
            num_scalar_prefetch=4, grid=(n_tiles, nf),
            in_specs=[pl.BlockSpec(memory_space=pl.ANY),
                      pl.BlockSpec((None, None, D_MODEL, tf),
                                   lambda i, j, te, nv, s, d: (lyr, te[i], 0, fj(i, j, nv))),
                      pl.BlockSpec((None, None, D_MODEL, tf),
                                   lambda i, j, te, nv, s, d: (lyr, te[i], 0, nf + fj(i, j, nv))),
                      pl.BlockSpec((None, None, tf, D_MODEL),
                                   lambda i, j, te, nv, s, d: (lyr, te[i], fj(i, j, nv), 0))],
            out_specs=pl.BlockSpec(memory_space=pl.ANY),
            scratch_shapes=[pltpu.VMEM((2, tm, D_MODEL), F32), pltpu.VMEM((tm, D_MODEL), BF16),
                            pltpu.VMEM((tm, D_MODEL), F32), pltpu.VMEM((2, tm, D_MODEL), F32),
                            pltpu.SemaphoreType.DMA((2,)), pltpu.SemaphoreType.DMA((2,))]),
        out_shape=jax.ShapeDtypeStruct((n_out, D_MODEL), F32),
        compiler_params=_params(("arbitrary", "arbitrary")), name="moe_expert_ffn",
    )(te, n_valid.reshape(1), src, dst, x2d, w_in, w_in, w_out)

    ct = 256
    nct = t // ct
    fulls = [ln_g.reshape(1, -1), ln_b.reshape(1, -1), w_proj.astype(BF16), w_gate.astype(BF16)]
    row = lambda c: pl.BlockSpec((ct, c), lambda i: (i, 0))
    return pl.pallas_call(
        _moe_combine_body, grid=(nct,),
        in_specs=[row(D_MODEL), row(D_MODEL), pl.BlockSpec((ct, D_MODEL), lambda i: (nct + i, 0)),
                  row(LANES), row(PLE_DIM)] + [_full_spec(a, 1) for a in fulls],
        out_specs=row(D_MODEL), out_shape=jax.ShapeDtypeStruct((t, D_MODEL), F32),
        compiler_params=_params(("arbitrary",)), name="moe_combine_postnorm_ple",
    )(x2d, y, y, meta, p2d, *fulls)


def kernel(x, p, rg_w_in, rg_conv_w, rg_conv_b, rg_w_a, rg_b_a, rg_w_x, rg_b_x, rg_lambda, rg_w_out,
           hg_w_in, hg_lb_logits, hg_w_out, ret_w_in, ret_w_out, gla_w_in, gla_w_gate, gla_b_gate,
           gla_w_out, dense_w_in, dense_w_out, moe_w_router, moe_w_in, moe_w_out, ple_w_proj,
           ple_w_gate, ln_mix_g, ln_mix_b, ln_ffn_g, ln_ffn_b):
    bsz, seq, d = x.shape
    t = bsz * seq
    depth = p.shape[0]
    moe_w_in_bf = moe_w_in.astype(BF16)
    moe_w_out_bf = moe_w_out.astype(BF16)
    lb_sm = jax.nn.softmax(hg_lb_logits.astype(F32), axis=0)
    lower_bounds = jnp.cumsum(lb_sm, axis=0) - lb_sm[0:1]
    h = x.reshape(t, d)
    for i in range(depth):
        kind, j = i % 4, i // 4
        if kind == 0:
            h = _rg_layer(h, bsz, seq, rg_w_in[j], rg_conv_w[j], rg_conv_b[j], rg_w_a[j], rg_b_a[j],
                          rg_w_x[j], rg_b_x[j], rg_lambda[j], rg_w_out[j], ln_mix_g[i], ln_mix_b[i])
        elif kind == 1:
            h = _hgrn2_layer(h, bsz, seq, hg_w_in[j], lower_bounds[i], hg_w_out[j], ln_mix_g[i], ln_mix_b[i])
        elif kind == 2:
            h = _ret_layer(h, bsz, seq, ret_w_in[j], ret_w_out[j], ln_mix_g[i], ln_mix_b[i])
        else:
            h = _gla_layer(h, bsz, seq, gla_w_in[j], gla_w_gate[j], gla_b_gate[j], gla_w_out[j],
                           ln_mix_g[i], ln_mix_b[i])
        p2d = p[i].reshape(t, p.shape[-1])
        if i % 2 == 0:
            h = _dense_ffn_layer(h, p2d, dense_w_in[i // 2], dense_w_out[i // 2], ln_ffn_g[i], ln_ffn_b[i],
                                 ple_w_proj[i], ple_w_gate[i])
        else:
            h = _moe_layer(h, p2d, moe_w_router[i // 2], moe_w_in_bf, moe_w_out_bf, i // 2,
                           ln_ffn_g[i], ln_ffn_b[i], ple_w_proj[i], ple_w_gate[i])
    return h.reshape(bsz, seq, d)
```

```python
import functools
import math

import jax
import jax.numpy as jnp
from jax import lax
from jax.experimental import pallas as pl
from jax.experimental.pallas import tpu as pltpu

F32 = jnp.float32
BF16 = jnp.bfloat16
I32 = jnp.int32

D_MODEL = 1024
DEPTH = 4
ALPHA = (2.0 * DEPTH) ** 0.25
LN_EPS = 1e-5
RG_WIDTH = D_MODEL
RG_BLOCK = 256
RG_BLOCKS = RG_WIDTH // RG_BLOCK
RG_CONV = 4
RG_C = 8.0
HG_HEADS = 8
HG_DK = D_MODEL // HG_HEADS
HG_DV = D_MODEL // HG_HEADS
HG_KDIM = HG_HEADS * HG_DK
HG_VDIM = HG_HEADS * HG_DV
RET_HEADS = 4
RET_DK = D_MODEL // RET_HEADS
RET_DV = 2 * D_MODEL // RET_HEADS
RET_KDIM = RET_HEADS * RET_DK
RET_VDIM = RET_HEADS * RET_DV
ROPE_BASE = 10000.0
GLA_HEADS = 4
GLA_DK = D_MODEL // 2 // GLA_HEADS
GLA_DV = D_MODEL // GLA_HEADS
GLA_KDIM = GLA_HEADS * GLA_DK
GLA_VDIM = GLA_HEADS * GLA_DV
GLA_RANK = 16
GLA_TAU = 16.0
GATE_CHUNK = 32
N_EXPERTS = 8
FFN_EXPERT = 3584
PLE_DIM = 256

LANES = 128
SUBLANES = 8
VMEM_LIMIT = 56 << 20

ROW_TILE = 512
RG_TILE = 512
GLA_TILE = 256
RET_CHUNK = 256
MOE_TILE = 512
MOE_NF = 2


def _dot(a, b):
    return jnp.dot(a, b, preferred_element_type=F32)


def _dot_nt(a, b):
    return lax.dot_general(a, b, (((1,), (1,)), ((), ())), preferred_element_type=F32)


def _dot_tn(a, b):
    return lax.dot_general(a, b, (((0,), (0,)), ((), ())), preferred_element_type=F32)


def _sigmoid(v):
    return jax.nn.sigmoid(v)


def _silu(v):
    return v * jax.nn.sigmoid(v)


def _softplus(v):
    return jnp.maximum(v, 0.0) + jnp.log1p(jnp.exp(-jnp.abs(v)))


def _layer_norm(v, g, b):
    mu = jnp.mean(v, -1, keepdims=True)
    d = v - mu
    var = jnp.mean(d * d, -1, keepdims=True)
    return d * lax.rsqrt(var + LN_EPS) * g + b


def _params(sem):
    return pltpu.CompilerParams(dimension_semantics=sem, vmem_limit_bytes=VMEM_LIMIT)


def _full_spec(a, n_grid):
    nd = a.ndim
    if n_grid == 1:
        return pl.BlockSpec(a.shape, lambda i: (0,) * nd)
    return pl.BlockSpec(a.shape, lambda i, j: (0,) * nd)


def _row_call(body, row_ins, full_ins, outs, name, tm=ROW_TILE):
    t = row_ins[0].shape[0]
    assert t % tm == 0
    in_specs = [pl.BlockSpec((tm, a.shape[1]), lambda i: (i, 0)) for a in row_ins]
    in_specs += [_full_spec(a, 1) for a in full_ins]
    out_shape = [jax.ShapeDtypeStruct((t, c), dt) for c, dt in outs]
    out_specs = [pl.BlockSpec((tm, c), lambda i: (i, 0)) for c, dt in outs]
    return pl.pallas_call(
        body, grid=(t // tm,), in_specs=in_specs, out_specs=out_specs, out_shape=out_shape,
        compiler_params=_params(("arbitrary",)), name=name)(*row_ins, *full_ins)


def _rg_body(x_ref, win_ref, cw_ref, cb_ref, wa_ref, ba_ref, wx_ref, bx_ref, lam_ref, wout_ref,
             g_ref, b_ref, o_ref, prev_ref, hc_ref, a_ref, i_ref, h_ref):
    tc = x_ref.shape[0]
    w = RG_WIDTH

    @pl.when(pl.program_id(1) == 0)
    def _():
        prev_ref[...] = jnp.zeros_like(prev_ref)
        hc_ref[...] = jnp.zeros_like(hc_ref)

    x = x_ref[...]
    y = _dot(x.astype(BF16), win_ref[...])
    gate = jax.nn.gelu(y[:, :w], approximate=True)
    rec = y[:, w:]

    ext = jnp.concatenate([prev_ref[...], rec], axis=0)
    cw = cw_ref[...]
    u = cb_ref[...] + cw[0:1, :] * ext[SUBLANES - 3:SUBLANES - 3 + tc, :]
    for j in range(1, RG_CONV):
        off = SUBLANES - (RG_CONV - 1) + j
        u = u + cw[j:j + 1, :] * ext[off:off + tc, :]
    prev_ref[...] = rec[tc - SUBLANES:, :]

    ra, rx = [], []
    for n in range(RG_BLOCKS):
        ub = u[:, n * RG_BLOCK:(n + 1) * RG_BLOCK].astype(BF16)
        ra.append(_dot(ub, wa_ref[n]))
        rx.append(_dot(ub, wx_ref[n]))
    r = _sigmoid(jnp.concatenate(ra, axis=1) + ba_ref[...])
    ig = _sigmoid(jnp.concatenate(rx, axis=1) + bx_ref[...])
    log_a = (-RG_C * _softplus(-lam_ref[...])) * r
    th = jnp.tanh(log_a)
    a_ref[...] = jnp.exp(log_a)
    i_ref[...] = jnp.sqrt(-2.0 * th / (1.0 - th)) * (ig * u)

    row = lax.broadcasted_iota(I32, (SUBLANES, w), 0)

    def group(gi, hc):
        r0 = pl.multiple_of(gi * SUBLANES, SUBLANES)
        a8 = a_ref[pl.ds(r0, SUBLANES), :]
        b8 = i_ref[pl.ds(r0, SUBLANES), :]
        for d in (1, 2, 4):
            m = row >= d
            a_sh = pltpu.roll(a8, d, 0)
            b_sh = pltpu.roll(b8, d, 0)
            b8 = jnp.where(m, a8 * b_sh + b8, b8)
            a8 = jnp.where(m, a8 * a_sh, a8)
        h8 = a8 * hc + b8
        h_ref[pl.ds(r0, SUBLANES), :] = h8
        return h8[SUBLANES - 1:SUBLANES, :]

    hc_ref[...] = lax.fori_loop(0, tc // SUBLANES, group, hc_ref[...])

    hg = (h_ref[...] * gate).astype(BF16)
    mix = _dot(hg, wout_ref[...])
    o_ref[...] = _layer_norm(ALPHA * x + mix, g_ref[...], b_ref[...])


def _rg_layer(x2d, bsz, seq, w_in, conv_w, conv_b, w_a, b_a, w_x, b_x, lam, w_out, ln_g, ln_b):
    tc = RG_TILE
    ns = seq // tc
    row = lambda v: v.reshape(1, -1)
    fulls = [w_in.astype(BF16), conv_w, row(conv_b), w_a.astype(BF16), row(b_a), w_x.astype(BF16),
             row(b_x), row(lam), w_out.astype(BF16), row(ln_g), row(ln_b)]
    tile = pl.BlockSpec((tc, D_MODEL), lambda b, s: (b * ns + s, 0))
    return pl.pallas_call(
        _rg_body, grid=(bsz, ns),
        in_specs=[tile] + [_full_spec(a, 2) for a in fulls],
        out_specs=tile, out_shape=jax.ShapeDtypeStruct(x2d.shape, F32),
        scratch_shapes=[pltpu.VMEM((SUBLANES, RG_WIDTH), F32), pltpu.VMEM((1, RG_WIDTH), F32),
                        pltpu.VMEM((tc, RG_WIDTH), F32), pltpu.VMEM((tc, RG_WIDTH), F32),
                        pltpu.VMEM((tc, RG_WIDTH), F32)],
        compiler_params=_params(("arbitrary", "arbitrary")), name="rglru_mixer")(x2d, *fulls)


def _ple(x2, p_ref, wp_ref, wgate_ref):
    gate = _sigmoid(_dot(x2.astype(BF16), wgate_ref[...]))
    return x2 + _dot(p_ref[...].astype(BF16), wp_ref[...]) * gate


def _dense_ffn_body(x_ref, p_ref, wg_ref, wu_ref, wo_ref, g_ref, b_ref, wp_ref, wgate_ref, o_ref):
    x = x_ref[...]
    xb = x.astype(BF16)
    h = (_silu(_dot(xb, wg_ref[...])) * _dot(xb, wu_ref[...])).astype(BF16)
    x2 = _layer_norm(ALPHA * x + _dot(h, wo_ref[...]), g_ref[...], b_ref[...])
    o_ref[...] = _ple(x2, p_ref, wp_ref, wgate_ref)


def _dense_ffn_layer(x2d, p2d, w_in, w_out, ln_g, ln_b, w_proj, w_gate):
    f = w_out.shape[0]
    w_in = w_in.astype(BF16)
    fulls = [w_in[:, :f], w_in[:, f:], w_out.astype(BF16), ln_g.reshape(1, -1), ln_b.reshape(1, -1),
             w_proj.astype(BF16), w_gate.astype(BF16)]
    return _row_call(_dense_ffn_body, [x2d, p2d], fulls, [(D_MODEL, F32)], "dense_swiglu_ple", tm=256)[0]


def _out_ln_body(o_ref, x_ref, w_ref, g_ref, b_ref, y_ref):
    y_ref[...] = _layer_norm(ALPHA * x_ref[...] + _dot(o_ref[...], w_ref[...]), g_ref[...], b_ref[...])


def _out_ln(o2d, x2d, w_out, ln_g, ln_b):
    fulls = [w_out.astype(BF16), ln_g.reshape(1, -1), ln_b.reshape(1, -1)]
    return _row_call(_out_ln_body, [o2d, x2d], fulls, [(D_MODEL, F32)], "mixer_out_postnorm")[0]


def _gla_body(q_ref, k_ref, v_ref, lg_ref, gate_ref, o_ref, cum_ref, st_ref, *, heads, dk, dv):
    bsz, tb, _ = q_ref.shape
    c = GATE_CHUNK

    @pl.when(pl.program_id(0) == 0)
    def _():
        st_ref[...] = jnp.zeros_like(st_ref)

    rmod = lax.broadcasted_iota(I32, (tb, heads * dk), 0) % c
    for b in range(bsz):
        cum = lg_ref[b]
        d = 1
        while d < c:
            cum = cum + jnp.where(rmod >= d, pltpu.roll(cum, d, 0), 0.0)
            d *= 2
        cum_ref[b] = cum

    ti = lax.broadcasted_iota(I32, (c, c), 0)
    si = lax.broadcasted_iota(I32, (c, c), 1)
    causal = ti >= si

    def chunk(ci, carry):
        r0 = pl.multiple_of(ci * c, c)
        rows = pl.ds(r0, c)
        for b in range(bsz):
            for h in range(heads):
                kc = slice(h * dk, (h + 1) * dk)
                vc = slice(h * dv, (h + 1) * dv)
                cum = cum_ref[b, rows, kc]
                ref = cum[c // 2:c // 2 + 1, :]
                last = cum[c - 1:c, :]
                q = q_ref[b, rows, kc].astype(F32)
                k = k_ref[b, rows, kc].astype(F32)
                v = v_ref[b, rows, vc]
                st = st_ref[b, h]
                inter = _dot_nt((q * jnp.exp(cum)).astype(BF16), st.astype(BF16))
                sc = _dot_nt((q * jnp.exp(cum - ref)).astype(BF16),
                             (k * jnp.exp(ref - cum)).astype(BF16))
                sc = jnp.where(causal, sc, 0.0)
                o = inter + _dot(sc.astype(BF16), v)
                st_ref[b, h] = st * jnp.exp(last) + _dot_tn(v, (k * jnp.exp(last - cum)).astype(BF16))
                o = o * lax.rsqrt(jnp.mean(o * o, -1, keepdims=True) + LN_EPS)
                o_ref[b, rows, vc] = (o * gate_ref[b, rows, vc].astype(F32)).astype(BF16)
        return carry

    lax.fori_loop(0, tb // c, chunk, 0, unroll=4)


def _gla_core(q, k, v, lg, gate, bsz, seq, heads, dk, dv):
    tb = GLA_TILE
    r3 = lambda a: a.reshape(bsz, seq, a.shape[-1])
    ins = [r3(q), r3(k), r3(v), r3(lg), r3(gate)]
    spec = lambda a: pl.BlockSpec((bsz, tb, a.shape[-1]), lambda s: (0, s, 0))
    out = pl.pallas_call(
        functools.partial(_gla_body, heads=heads, dk=dk, dv=dv), grid=(seq // tb,),
        in_specs=[spec(a) for a in ins], out_specs=spec(ins[2]),
        out_shape=jax.ShapeDtypeStruct((bsz, seq, heads * dv), BF16),
        scratch_shapes=[pltpu.VMEM((bsz, tb, heads * dk), F32), pltpu.VMEM((bsz, heads, dv, dk), F32)],
        compiler_params=_params(("arbitrary",)), name="gated_linear_attention")(*ins)
    return out.reshape(bsz * seq, heads * dv)


def _hg_in_body(x_ref, w_ref, lb_ref, q_ref, k_ref, v_ref, lg_ref, gs_ref):
    y = _dot(x_ref[...].astype(BF16), w_ref[...])
    lb = lb_ref[...]
    f = lb + (1.0 - lb) * _sigmoid(y[:, HG_KDIM:2 * HG_KDIM])
    q_ref[...] = _silu(y[:, :HG_KDIM]).astype(BF16)
    k_ref[...] = (1.0 - f).astype(BF16)
    lg_ref[...] = jnp.log(f)
    v_ref[...] = y[:, 2 * HG_KDIM:2 * HG_KDIM + HG_VDIM].astype(BF16)
    gs_ref[...] = _silu(y[:, 2 * HG_KDIM + HG_VDIM:]).astype(BF16)


def _hgrn2_layer(x2d, bsz, seq, w_in, lb, w_out, ln_g, ln_b):
    outs = [(HG_KDIM, BF16), (HG_KDIM, BF16), (HG_VDIM, BF16), (HG_KDIM, F32), (HG_VDIM, BF16)]
    q, k, v, lg, gs = _row_call(_hg_in_body, [x2d], [w_in.astype(BF16), lb.reshape(1, -1)], outs,
                                "hgrn2_in_proj")
    o = _gla_core(q, k, v, lg, gs, bsz, seq, HG_HEADS, HG_DK, HG_DV)
    return _out_ln(o, x2d, w_out, ln_g, ln_b)


def _gla_in_body(x_ref, w_ref, wl_ref, wg_ref, bg_ref, q_ref, k_ref, v_ref, lg_ref, gs_ref):
    xb = x_ref[...].astype(BF16)
    y = _dot(xb, w_ref[...])
    gl = _dot(xb, wl_ref[...])
    z = _dot(gl.astype(BF16), wg_ref[...]) + bg_ref[...]
    lg_ref[...] = (jnp.minimum(z, 0.0) - jnp.log1p(jnp.exp(-jnp.abs(z)))) * (1.0 / GLA_TAU)
    q_ref[...] = (y[:, :GLA_KDIM] * (GLA_DK ** -0.5)).astype(BF16)
    k_ref[...] = y[:, GLA_KDIM:2 * GLA_KDIM].astype(BF16)
    v_ref[...] = y[:, 2 * GLA_KDIM:2 * GLA_KDIM + GLA_VDIM].astype(BF16)
    gs_ref[...] = _silu(y[:, 2 * GLA_KDIM + GLA_VDIM:]).astype(BF16)


def _gla_layer(x2d, bsz, seq, w_in, w_gate, b_gate, w_out, ln_g, ln_b):
    n_main = 2 * GLA_KDIM + 2 * GLA_VDIM
    w_main = w_in[:, :n_main].astype(BF16)
    w_low = jnp.pad(w_in[:, n_main:], ((0, 0), (0, LANES - GLA_RANK))).astype(BF16)
    w_gate_p = jnp.pad(w_gate, ((0, LANES - GLA_RANK), (0, 0))).astype(BF16)
    outs = [(GLA_KDIM, BF16), (GLA_KDIM, BF16), (GLA_VDIM, BF16), (GLA_KDIM, F32), (GLA_VDIM, BF16)]
    q, k, v, lg, gs = _row_call(_gla_in_body, [x2d], [w_main, w_low, w_gate_p, b_gate.reshape(1, -1)],
                                outs, "gla_in_proj")
    o = _gla_core(q, k, v, lg, gs, bsz, seq, GLA_HEADS, GLA_DK, GLA_DV)
    return _out_ln(o, x2d, w_out, ln_g, ln_b)


def _ret_in_body(x_ref, cos_ref, sin_ref, w_ref, q_ref, k_ref, v_ref, gs_ref):
    y = _dot(x_ref[...].astype(BF16), w_ref[...])
    cos = cos_ref[...]
    sin = sin_ref[...]
    half = RET_DK // 2

    def rot(base, scale, dst):
        for h in range(RET_HEADS):
            t1 = y[:, base + h * RET_DK:base + h * RET_DK + half]
            t2 = y[:, base + h * RET_DK + half:base + (h + 1) * RET_DK]
            dst[:, h * RET_DK:h * RET_DK + half] = ((t1 * cos - t2 * sin) * scale).astype(BF16)
            dst[:, h * RET_DK + half:(h + 1) * RET_DK] = ((t1 * sin + t2 * cos) * scale).astype(BF16)

    rot(0, 1.0, q_ref)
    rot(RET_KDIM, RET_DK ** -0.5, k_ref)
    v_ref[...] = y[:, 2 * RET_KDIM:2 * RET_KDIM + RET_VDIM].astype(BF16)
    gs_ref[...] = _silu(y[:, 2 * RET_KDIM + RET_VDIM:]).astype(BF16)


def _ret_body(q_ref, k_ref, v_ref, gs_ref, dq_ref, dk_ref, dm_ref, gc_ref, o_ref, st_ref):
    bsz = q_ref.shape[0]

    @pl.when(pl.program_id(0) == 0)
    def _():
        st_ref[...] = jnp.zeros_like(st_ref)

    for b in range(bsz):
        for h in range(RET_HEADS):
            kc = slice(h * RET_DK, (h + 1) * RET_DK)
            vc = slice(h * RET_DV, (h + 1) * RET_DV)
            q = q_ref[b, :, kc]
            k = k_ref[b, :, kc]
            v = v_ref[b, :, vc]
            st = st_ref[b, h]
            inter = _dot(q, st.astype(BF16)) * dq_ref[h]
            sc = _dot_nt(q, k) * dm_ref[h]
            o = inter + _dot(sc.astype(BF16), v)
            kd = (k.astype(F32) * dk_ref[h]).astype(BF16)
            st_ref[b, h] = st * gc_ref[h] + _dot_tn(kd, v)
            mu = jnp.mean(o, -1, keepdims=True)
            dlt = o - mu
            o = dlt * lax.rsqrt(jnp.mean(dlt * dlt, -1, keepdims=True) + LN_EPS)
            o_ref[b, :, vc] = (gs_ref[b, :, vc].astype(F32) * o).astype(BF16)


def _ret_layer(x2d, bsz, seq, w_in, w_out, ln_g, ln_b):
    half = RET_DK // 2
    inv = ROPE_BASE ** (-jnp.arange(0, RET_DK, 2, dtype=F32) / RET_DK)
    ang = jnp.arange(seq, dtype=F32)[:, None] * inv[None, :]
    cos, sin = jnp.cos(ang), jnp.sin(ang)
    tm = ROW_TILE
    ns = seq // tm
    t = x2d.shape[0]
    w = w_in.astype(BF16)
    outs = [(RET_KDIM, BF16), (RET_KDIM, BF16), (RET_VDIM, BF16), (RET_VDIM, BF16)]
    q, k, v, gs = pl.pallas_call(
        _ret_in_body, grid=(t // tm,),
        in_specs=[pl.BlockSpec((tm, D_MODEL), lambda i: (i, 0)),
                  pl.BlockSpec((tm, half), lambda i: (i % ns, 0)),
                  pl.BlockSpec((tm, half), lambda i: (i % ns, 0)),
                  _full_spec(w, 1)],
        out_specs=[pl.BlockSpec((tm, c), lambda i: (i, 0)) for c, _ in outs],
        out_shape=[jax.ShapeDtypeStruct((t, c), dt) for c, dt in outs],
        compiler_params=_params(("arbitrary",)), name="retention_in_proj")(x2d, cos, sin, w)

    c = RET_CHUNK
    pos = jnp.arange(c, dtype=F32)
    lgam = jnp.log1p(-jnp.exp2(-5.0 - jnp.arange(RET_HEADS, dtype=F32)))[:, None]
    decay_q = jnp.exp(lgam * (pos + 1.0))[:, :, None]
    decay_k = jnp.exp(lgam * (c - 1.0 - pos))[:, :, None]
    rel = pos[:, None] - pos[None, :]
    dmat = jnp.where(rel >= 0, jnp.exp(lgam[:, :, None] * jnp.maximum(rel, 0.0)), 0.0)
    gchunk = jnp.broadcast_to(jnp.exp(lgam * c)[:, :, None], (RET_HEADS, 1, LANES))
    decay_q = jnp.broadcast_to(decay_q, (RET_HEADS, c, LANES))
    decay_k = jnp.broadcast_to(decay_k, (RET_HEADS, c, LANES))
    r3 = lambda a: a.reshape(bsz, seq, a.shape[-1])
    ins = [r3(q), r3(k), r3(v), r3(gs)]
    tabs = [decay_q[:, :, :1], decay_k[:, :, :1], dmat, gchunk[:, :, :1]]
    spec = lambda a: pl.BlockSpec((bsz, c, a.shape[-1]), lambda s: (0, s, 0))
    o = pl.pallas_call(
        _ret_body, grid=(seq // c,),
        in_specs=[spec(a) for a in ins] + [_full_spec(a, 1) for a in tabs],
        out_specs=spec(ins[2]), out_shape=jax.ShapeDtypeStruct((bsz, seq, RET_VDIM), BF16),
        scratch_shapes=[pltpu.VMEM((bsz, RET_HEADS, RET_DK, RET_DV), F32)],
        compiler_params=_params(("arbitrary",)), name="retention")(*ins, *tabs)
    return _out_ln(o.reshape(t, RET_VDIM), x2d, w_out, ln_g, ln_b)


def _router_body(x_ref, wh_ref, wl_ref, meta_ref, cnt_ref, carry_ref):
    tm = x_ref.shape[0]

    @pl.when(pl.program_id(0) == 0)
    def _():
        carry_ref[...] = jnp.zeros_like(carry_ref)

    x = x_ref[...]
    xh = x.astype(BF16)
    xl = (x - xh.astype(F32)).astype(BF16)
    logits = _dot(xh, wh_ref[...]) + (_dot(xl, wh_ref[...]) + _dot(xh, wl_ref[...]))
    lane = lax.broadcasted_iota(I32, (tm, LANES), 1).astype(F32)
    neg = jnp.float32(-jnp.inf)
    big = jnp.float32(LANES)
    m0 = jnp.where(lane < N_EXPERTS, logits, neg)
    v1 = jnp.max(m0, -1, keepdims=True)
    i1 = jnp.min(jnp.where(m0 == v1, lane, big), -1, keepdims=True)
    m1 = jnp.where(lane == i1, neg, m0)
    v2 = jnp.max(m1, -1, keepdims=True)
    i2 = jnp.min(jnp.where(m1 == v2, lane, big), -1, keepdims=True)
    e2 = jnp.exp(v2 - v1)
    w1 = 1.0 / (1.0 + e2)
    w2 = e2 / (1.0 + e2)
    oh1 = (lane == i1).astype(F32)
    oh2 = (lane == i2).astype(F32)
    oh = oh1 + oh2
    ti = lax.broadcasted_iota(I32, (tm, tm), 0)
    si = lax.broadcasted_iota(I32, (tm, tm), 1)
    before = _dot((ti > si).astype(BF16), oh.astype(BF16)) + carry_ref[...]
    r1 = jnp.sum(before * oh1, -1, keepdims=True)
    r2 = jnp.sum(before * oh2, -1, keepdims=True)
    carry_ref[...] += jnp.sum(oh, 0, keepdims=True)
    cnt_ref[...] = jnp.broadcast_to(carry_ref[...], cnt_ref.shape)
    meta = jnp.zeros((tm, LANES), F32)
    for col, val in enumerate((i1, i2, w1, w2, r1, r2)):
        meta = jnp.where(lane == col, val, meta)
    meta_ref[...] = meta


def _moe_ffn_body(te_ref, nv_ref, src_ref, dst_ref, x_hbm, wg_ref, wu_ref, wo_ref, y_hbm,
                  xbuf, xs, acc, ybuf, gsem, ssem, *, n_tok_rows):
    tm = xs.shape[0]
    i = pl.program_id(0)
    j = pl.program_id(1)
    nf = pl.num_programs(1)
    rs = tm // MOE_NF
    nv = nv_ref[0]
    valid = i < nv
    slot = i % 2
    other = 1 - slot

    def gather_row(buf, r, tok):
        return pltpu.make_async_copy(x_hbm.at[pl.ds(tok, 1), :], xbuf.at[buf, pl.ds(r, 1), :], gsem.at[buf])

    def scatter_row(buf, r, pos):
        return pltpu.make_async_copy(ybuf.at[buf, pl.ds(r, 1), :], y_hbm.at[pl.ds(pos, 1), :], ssem.at[buf])

    def gather_tile(buf):
        return pltpu.make_async_copy(x_hbm.at[pl.ds(0, tm), :], xbuf.at[buf], gsem.at[buf])

    def scatter_tile(buf):
        return pltpu.make_async_copy(ybuf.at[buf], y_hbm.at[pl.ds(0, tm), :], ssem.at[buf])

    @pl.when((i == 0) & (j == 0))
    def _():
        def issue(r, c):
            gather_row(0, r, src_ref[r]).start()
            return c
        lax.fori_loop(0, tm, issue, 0)
        ybuf[1] = jnp.zeros(ybuf.shape[1:], ybuf.dtype)
        n_spare_tiles = (y_hbm.shape[0] - n_tok_rows) // tm
        for k in range(n_spare_tiles):
            pltpu.make_async_copy(ybuf.at[1], y_hbm.at[pl.ds(n_tok_rows + k * tm, tm), :], ssem.at[1]).start()
        for k in range(n_spare_tiles):
            scatter_tile(1).wait()

    @pl.when(valid & (j == 0))
    def _():
        gather_tile(slot).wait()
        xs[...] = xbuf[slot].astype(BF16)

    @pl.when(valid)
    def _():
        for u in range(rs):
            r = j * rs + u
            gather_row(other, r, src_ref[(i + 1) * tm + r]).start()
            scatter_row(other, r, dst_ref[i * tm + r]).start()
        xb = xs[...]
        h = (_silu(_dot(xb, wg_ref[...])) * _dot(xb, wu_ref[...])).astype(BF16)
        part = _dot(h, wo_ref[...])

        @pl.when(j == 0)
        def _():
            acc[...] = part

        @pl.when(j > 0)
        def _():
            acc[...] += part

    @pl.when(valid & (j == nf - 1))
    def _():
        @pl.when(i >= 1)
        def _():
            scatter_tile(slot).wait()
        ybuf[slot] = acc[...]

    @pl.when((i == nv) & (j == 0))
    def _():
        gather_tile(slot).wait()
        scatter_tile(slot).wait()

        def issue(r, c):
            scatter_row(other, r, dst_ref[i * tm + r]).start()
            return c
        lax.fori_loop(0, tm, issue, 0)
        scatter_tile(other).wait()


def _row_index_body(d1_ref, d2_ref, out_ref):
    t = d1_ref.shape[0]

    def fill(p, c):
        out_ref[p] = -1
        return c

    def put(tok, c):
        out_ref[d1_ref[tok]] = tok
        out_ref[d2_ref[tok]] = t + tok
        return c

    lax.fori_loop(0, out_ref.shape[0], fill, 0, unroll=8)
    lax.fori_loop(0, t, put, 0, unroll=8)


def _moe_combine_body(x_ref, y1_ref, y2_ref, meta_ref, p_ref, g_ref, b_ref, wp_ref, wgate_ref, o_ref):
    meta = meta_ref[...]
    f = meta[:, 2:3] * y1_ref[...] + meta[:, 3:4] * y2_ref[...]
    x2 = _layer_norm(ALPHA * x_ref[...] + f, g_ref[...], b_ref[...])
    o_ref[...] = _ple(x2, p_ref, wp_ref, wgate_ref)


def _moe_layer(x2d, p2d, w_router, w_in, w_out, layer, ln_g, ln_b, w_proj, w_gate):
    t = x2d.shape[0]
    tm = MOE_TILE
    nf = MOE_NF
    tf = FFN_EXPERT // nf

    wr = jnp.pad(w_router, ((0, 0), (0, LANES - N_EXPERTS)))
    wr_hi = wr.astype(BF16)
    wr_lo = (wr - wr_hi.astype(F32)).astype(BF16)
    rt = ROW_TILE
    meta, cnt = pl.pallas_call(
        _router_body, grid=(t // rt,),
        in_specs=[pl.BlockSpec((rt, D_MODEL), lambda i: (i, 0)), _full_spec(wr_hi, 1), _full_spec(wr_lo, 1)],
        out_specs=[pl.BlockSpec((rt, LANES), lambda i: (i, 0)), pl.BlockSpec((SUBLANES, LANES), lambda i: (0, 0))],
        out_shape=[jax.ShapeDtypeStruct((t, LANES), F32), jax.ShapeDtypeStruct((SUBLANES, LANES), F32)],
        scratch_shapes=[pltpu.VMEM((1, LANES), F32)],
        compiler_params=_params(("arbitrary",)), name="moe_router")(x2d, wr_hi, wr_lo)

    n_tiles = (2 * t + N_EXPERTS * (tm - 1)) // tm + 1
    n_rows = n_tiles * tm
    counts = cnt[0, :N_EXPERTS].astype(I32)
    padded = ((counts + tm - 1) // tm) * tm
    ends = jnp.cumsum(padded)
    offs = ends - padded
    e1, e2 = meta[:, 0].astype(I32), meta[:, 1].astype(I32)
    d1 = offs[e1] + meta[:, 4].astype(I32)
    d2 = offs[e2] + meta[:, 5].astype(I32)
    slot_of_row = pl.pallas_call(
        _row_index_body,
        grid_spec=pltpu.PrefetchScalarGridSpec(
            num_scalar_prefetch=2, grid=(1,), in_specs=[],
            out_specs=pl.BlockSpec(memory_space=pltpu.SMEM)),
        out_shape=jax.ShapeDtypeStruct((n_rows,), I32),
        compiler_params=_params(("arbitrary",)), name="moe_row_index")(d1, d2)
    is_pad = slot_of_row < 0
    src = jnp.where(is_pad, 0, slot_of_row % t)
    spare = 2 * t + tm + jnp.cumsum(is_pad.astype(I32)) - 1
    dst = jnp.where(is_pad, spare, slot_of_row)
    src = jnp.concatenate([src, jnp.zeros((tm,), I32)])
    dst = jnp.concatenate([2 * t + jnp.arange(tm, dtype=I32), dst])
    assert (2 * t) % tm == 0
    n_out = 2 * t + tm + (n_rows - 2 * t)
    n_valid = (ends[-1] // tm).astype(I32)
    starts = jnp.arange(n_tiles, dtype=I32) * tm
    te = jnp.minimum(jnp.sum(ends[None, :] <= starts[:, None], axis=1), N_EXPERTS - 1).astype(I32)
    te = jnp.where(jnp.arange(n_tiles) < n_valid, te, te[jnp.maximum(n_valid - 1, 0)])

    def fj(i, j, nv):
        return jnp.where(i < nv[0], j, nf - 1)

    lyr = layer

    y = pl.pallas_call(
        functools.partial(_moe_ffn_body, n_tok_rows=2 * t),
        grid_spec=pltpu.PrefetchScalarGridSpec(
            num_scalar_prefetch=4, grid=(n_tiles, nf),
            in_specs=[pl.BlockSpec(memory_space=pl.ANY),
                      pl.BlockSpec((None, None, D_MODEL, tf),
                                   lambda i, j, te, nv, s, d: (lyr, te[i], 0, fj(i, j, nv))),
                      pl.BlockSpec((None, None, D_MODEL, tf),
                                   lambda i, j, te, nv, s, d: (lyr, te[i], 0, nf + fj(i, j, nv))),
                      pl.BlockSpec((None, None, tf, D_MODEL),
                                   lambda i, j, te, nv, s, d: (lyr, te[i], fj(i, j, nv), 0))],
            out_specs=pl.BlockSpec(memory_space=pl.ANY),
            scratch_shapes=[pltpu.VMEM((2, tm, D_MODEL), F32), pltpu.VMEM((tm, D_MODEL), BF16),
                            pltpu.VMEM((tm, D_MODEL), F32), pltpu.VMEM((2, tm, D_MODEL), F32),
                            pltpu.SemaphoreType.DMA((2,)), pltpu.SemaphoreType.DMA((2,))]),
        out_shape=jax.ShapeDtypeStruct((n_out, D_MODEL), F32),
        compiler_params=_params(("arbitrary", "arbitrary")), name="moe_expert_ffn",
    )(te, n_valid.reshape(1), src, dst, x2d, w_in, w_in, w_out)

    ct = 256
    nct = t // ct
    fulls = [ln_g.reshape(1, -1), ln_b.reshape(1, -1), w_proj.astype(BF16), w_gate.astype(BF16)]
    row = lambda c: pl.BlockSpec((ct, c), lambda i: (i, 0))
    return pl.pallas_call(
        _moe_combine_body, grid=(nct,),
        in_specs=[row(D_MODEL), row(D_MODEL), pl.BlockSpec((ct, D_MODEL), lambda i: (nct + i, 0)),
                  row(LANES), row(PLE_DIM)] + [_full_spec(a, 1) for a in fulls],
        out_specs=row(D_MODEL), out_shape=jax.ShapeDtypeStruct((t, D_MODEL), F32),
        compiler_params=_params(("arbitrary",)), name="moe_combine_postnorm_ple",
    )(x2d, y, y, meta, p2d, *fulls)


def kernel(x, p, rg_w_in, rg_conv_w, rg_conv_b, rg_w_a, rg_b_a, rg_w_x, rg_b_x, rg_lambda, rg_w_out,
           hg_w_in, hg_lb_logits, hg_w_out, ret_w_in, ret_w_out, gla_w_in, gla_w_gate, gla_b_gate,
           gla_w_out, dense_w_in, dense_w_out, moe_w_router, moe_w_in, moe_w_out, ple_w_proj,
           ple_w_gate, ln_mix_g, ln_mix_b, ln_ffn_g, ln_ffn_b):
    bsz, seq, d = x.shape
    t = bsz * seq
    depth = p.shape[0]
    moe_w_in_bf = moe_w_in.astype(BF16)
    moe_w_out_bf = moe_w_out.astype(BF16)
    lb_sm = jax.nn.softmax(hg_lb_logits.astype(F32), axis=0)
    lower_bounds = jnp.cumsum(lb_sm, axis=0) - lb_sm[0:1]
    h = x.reshape(t, d)
    for i in range(depth):
        kind, j = i % 4, i // 4
        if kind == 0:
            h = _rg_layer(h, bsz, seq, rg_w_in[j], rg_conv_w[j], rg_conv_b[j], rg_w_a[j], rg_b_a[j],
                          rg_w_x[j], rg_b_x[j], rg_lambda[j], rg_w_out[j], ln_mix_g[i], ln_mix_b[i])
        elif kind == 1:
            h = _hgrn2_layer(h, bsz, seq, hg_w_in[j], lower_bounds[i], hg_w_out[j], ln_mix_g[i], ln_mix_b[i])
        elif kind == 2:
            h = _ret_layer(h, bsz, seq, ret_w_in[j], ret_w_out[j], ln_mix_g[i], ln_mix_b[i])
        else:
            h = _gla_layer(h, bsz, seq, gla_w_in[j], gla_w_gate[j], gla_b_gate[j], gla_w_out[j],
                           ln_mix_g[i], ln_mix_b[i])
        p2d = p[i].reshape(t, p.shape[-1])
        if i % 2 == 0:
            h = _dense_ffn_layer(h, p2d, dense_w_in[i // 2], dense_w_out[i // 2], ln_ffn_g[i], ln_ffn_b[i],
                                 ple_w_proj[i], ple_w_gate[i])
        else:
            h = _moe_layer(h, p2d, moe_w_router[i // 2], moe_w_in_bf, moe_w_out_bf, i // 2,
                           ln_ffn_g[i], ln_ffn_b[i], ple_w_proj[i], ple_w_gate[i])
    return h.reshape(bsz, seq, d)
```

```python
import functools
import math

import jax
import jax.numpy as jnp
from jax import lax
from jax.experimental import pallas as pl
from jax.experimental.pallas import tpu as pltpu

F32 = jnp.float32
BF16 = jnp.bfloat16
I32 = jnp.int32

D_MODEL = 1024
DEPTH = 4
ALPHA = (2.0 * DEPTH) ** 0.25
LN_EPS = 1e-5
RG_WIDTH = D_MODEL
RG_BLOCK = 256
RG_BLOCKS = RG_WIDTH // RG_BLOCK
RG_CONV = 4
RG_C = 8.0
HG_HEADS = 8
HG_DK = D_MODEL // HG_HEADS
HG_DV = D_MODEL // HG_HEADS
HG_KDIM = HG_HEADS * HG_DK
HG_VDIM = HG_HEADS * HG_DV
RET_HEADS = 4
RET_DK = D_MODEL // RET_HEADS
RET_DV = 2 * D_MODEL // RET_HEADS
RET_KDIM = RET_HEADS * RET_DK
RET_VDIM = RET_HEADS * RET_DV
ROPE_BASE = 10000.0
GLA_HEADS = 4
GLA_DK = D_MODEL // 2 // GLA_HEADS
GLA_DV = D_MODEL // GLA_HEADS
GLA_KDIM = GLA_HEADS * GLA_DK
GLA_VDIM = GLA_HEADS * GLA_DV
GLA_RANK = 16
GLA_TAU = 16.0
GATE_CHUNK = 32
N_EXPERTS = 8
FFN_EXPERT = 3584
PLE_DIM = 256

LANES = 128
SUBLANES = 8
VMEM_LIMIT = 56 << 20

ROW_TILE = 512
RG_TILE = 512
GLA_TILE = 256
RET_CHUNK = 256
MOE_TILE = 512
MOE_NF = 2


def _dot(a, b):
    return jnp.dot(a, b, preferred_element_type=F32)


def _dot_nt(a, b):
    return lax.dot_general(a, b, (((1,), (1,)), ((), ())), preferred_element_type=F32)


def _dot_tn(a, b):
    return lax.dot_general(a, b, (((0,), (0,)), ((), ())), preferred_element_type=F32)


def _sigmoid(v):
    return jax.nn.sigmoid(v)


def _silu(v):
    return v * jax.nn.sigmoid(v)


def _softplus(v):
    return jnp.maximum(v, 0.0) + jnp.log1p(jnp.exp(-jnp.abs(v)))


def _layer_norm(v, g, b):
    mu = jnp.mean(v, -1, keepdims=True)
    d = v - mu
    var = jnp.mean(d * d, -1, keepdims=True)
    return d * lax.rsqrt(var + LN_EPS) * g + b


def _params(sem):
    return pltpu.CompilerParams(dimension_semantics=sem, vmem_limit_bytes=VMEM_LIMIT)


def _full_spec(a, n_grid):
    nd = a.ndim
    if n_grid == 1:
        return pl.BlockSpec(a.shape, lambda i: (0,) * nd)
    return pl.BlockSpec(a.shape, lambda i, j: (0,) * nd)


def _row_call(body, row_ins, full_ins, outs, name, tm=ROW_TILE):
    t = row_ins[0].shape[0]
    assert t % tm == 0
    in_specs = [pl.BlockSpec((tm, a.shape[1]), lambda i: (i, 0)) for a in row_ins]
    in_specs += [_full_spec(a, 1) for a in full_ins]
    out_shape = [jax.ShapeDtypeStruct((t, c), dt) for c, dt in outs]
    out_specs = [pl.BlockSpec((tm, c), lambda i: (i, 0)) for c, dt in outs]
    return pl.pallas_call(
        body, grid=(t // tm,), in_specs=in_specs, out_specs=out_specs, out_shape=out_shape,
        compiler_params=_params(("arbitrary",)), name=name)(*row_ins, *full_ins)


def _rg_body(x_ref, win_ref, cw_ref, cb_ref, wa_ref, ba_ref, wx_ref, bx_ref, lam_ref, wout_ref,
             g_ref, b_ref, o_ref, prev_ref, hc_ref, a_ref, i_ref, h_ref):
    tc = x_ref.shape[0]
    w = RG_WIDTH

    @pl.when(pl.program_id(1) == 0)
    def _():
        prev_ref[...] = jnp.zeros_like(prev_ref)
        hc_ref[...] = jnp.zeros_like(hc_ref)

    x = x_ref[...]
    y = _dot(x.astype(BF16), win_ref[...])
    gate = jax.nn.gelu(y[:, :w], approximate=True)
    rec = y[:, w:]

    ext = jnp.concatenate([prev_ref[...], rec], axis=0)
    cw = cw_ref[...]
    u = cb_ref[...] + cw[0:1, :] * ext[SUBLANES - 3:SUBLANES - 3 + tc, :]
    for j in range(1, RG_CONV):
        off = SUBLANES - (RG_CONV - 1) + j
        u = u + cw[j:j + 1, :] * ext[off:off + tc, :]
    prev_ref[...] = rec[tc - SUBLANES:, :]

    ra, rx = [], []
    for n in range(RG_BLOCKS):
        ub = u[:, n * RG_BLOCK:(n + 1) * RG_BLOCK].astype(BF16)
        ra.append(_dot(ub, wa_ref[n]))
        rx.append(_dot(ub, wx_ref[n]))
    r = _sigmoid(jnp.concatenate(ra, axis=1) + ba_ref[...])
    ig = _sigmoid(jnp.concatenate(rx, axis=1) + bx_ref[...])
    log_a = (-RG_C * _softplus(-lam_ref[...])) * r
    th = jnp.tanh(log_a)
    a_ref[...] = jnp.exp(log_a)
    i_ref[...] = jnp.sqrt(-2.0 * th / (1.0 - th)) * (ig * u)

    row = lax.broadcasted_iota(I32, (SUBLANES, w), 0)

    def group(gi, hc):
        r0 = pl.multiple_of(gi * SUBLANES, SUBLANES)
        a8 = a_ref[pl.ds(r0, SUBLANES), :]
        b8 = i_ref[pl.ds(r0, SUBLANES), :]
        for d in (1, 2, 4):
            m = row >= d
            a_sh = pltpu.roll(a8, d, 0)
            b_sh = pltpu.roll(b8, d, 0)
            b8 = jnp.where(m, a8 * b_sh + b8, b8)
            a8 = jnp.where(m, a8 * a_sh, a8)
        h8 = a8 * hc + b8
        h_ref[pl.ds(r0, SUBLANES), :] = h8
        return h8[SUBLANES - 1:SUBLANES, :]

    hc_ref[...] = lax.fori_loop(0, tc // SUBLANES, group, hc_ref[...])

    hg = (h_ref[...] * gate).astype(BF16)
    mix = _dot(hg, wout_ref[...])
    o_ref[...] = _layer_norm(ALPHA * x + mix, g_ref[...], b_ref[...])


def _rg_layer(x2d, bsz, seq, w_in, conv_w, conv_b, w_a, b_a, w_x, b_x, lam, w_out, ln_g, ln_b):
    tc = RG_TILE
    ns = seq // tc
    row = lambda v: v.reshape(1, -1)
    fulls = [w_in.astype(BF16), conv_w, row(conv_b), w_a.astype(BF16), row(b_a), w_x.astype(BF16),
             row(b_x), row(lam), w_out.astype(BF16), row(ln_g), row(ln_b)]
    tile = pl.BlockSpec((tc, D_MODEL), lambda b, s: (b * ns + s, 0))
    return pl.pallas_call(
        _rg_body, grid=(bsz, ns),
        in_specs=[tile] + [_full_spec(a, 2) for a in fulls],
        out_specs=tile, out_shape=jax.ShapeDtypeStruct(x2d.shape, F32),
        scratch_shapes=[pltpu.VMEM((SUBLANES, RG_WIDTH), F32), pltpu.VMEM((1, RG_WIDTH), F32),
                        pltpu.VMEM((tc, RG_WIDTH), F32), pltpu.VMEM((tc, RG_WIDTH), F32),
                        pltpu.VMEM((tc, RG_WIDTH), F32)],
        compiler_params=_params(("arbitrary", "arbitrary")), name="rglru_mixer")(x2d, *fulls)


def _ple(x2, p_ref, wp_ref, wgate_ref):
    gate = _sigmoid(_dot(x2.astype(BF16), wgate_ref[...]))
    return x2 + _dot(p_ref[...].astype(BF16), wp_ref[...]) * gate


def _dense_ffn_body(x_ref, p_ref, wg_ref, wu_ref, wo_ref, g_ref, b_ref, wp_ref, wgate_ref, o_ref):
    x = x_ref[...]
    xb = x.astype(BF16)
    h = (_silu(_dot(xb, wg_ref[...])) * _dot(xb, wu_ref[...])).astype(BF16)
    x2 = _layer_norm(ALPHA * x + _dot(h, wo_ref[...]), g_ref[...], b_ref[...])
    o_ref[...] = _ple(x2, p_ref, wp_ref, wgate_ref)


def _dense_ffn_layer(x2d, p2d, w_in, w_out, ln_g, ln_b, w_proj, w_gate):
    f = w_out.shape[0]
    w_in = w_in.astype(BF16)
    fulls = [w_in[:, :f], w_in[:, f:], w_out.astype(BF16), ln_g.reshape(1, -1), ln_b.reshape(1, -1),
             w_proj.astype(BF16), w_gate.astype(BF16)]
    return _row_call(_dense_ffn_body, [x2d, p2d], fulls, [(D_MODEL, F32)], "dense_swiglu_ple", tm=256)[0]


def _out_ln_body(o_ref, x_ref, w_ref, g_ref, b_ref, y_ref):
    y_ref[...] = _layer_norm(ALPHA * x_ref[...] + _dot(o_ref[...], w_ref[...]), g_ref[...], b_ref[...])


def _out_ln(o2d, x2d, w_out, ln_g, ln_b):
    fulls = [w_out.astype(BF16), ln_g.reshape(1, -1), ln_b.reshape(1, -1)]
    return _row_call(_out_ln_body, [o2d, x2d], fulls, [(D_MODEL, F32)], "mixer_out_postnorm")[0]


def _gla_body(q_ref, k_ref, v_ref, lg_ref, gate_ref, o_ref, cum_ref, st_ref, *, heads, dk, dv):
    bsz, tb, _ = q_ref.shape
    c = GATE_CHUNK

    @pl.when(pl.program_id(0) == 0)
    def _():
        st_ref[...] = jnp.zeros_like(st_ref)

    rmod = lax.broadcasted_iota(I32, (tb, heads * dk), 0) % c
    for b in range(bsz):
        cum = lg_ref[b]
        d = 1
        while d < c:
            cum = cum + jnp.where(rmod >= d, pltpu.roll(cum, d, 0), 0.0)
            d *= 2
        cum_ref[b] = cum

    ti = lax.broadcasted_iota(I32, (c, c), 0)
    si = lax.broadcasted_iota(I32, (c, c), 1)
    causal = ti >= si

    def chunk(ci, carry):
        r0 = pl.multiple_of(ci * c, c)
        rows = pl.ds(r0, c)
        for b in range(bsz):
            for h in range(heads):
                kc = slice(h * dk, (h + 1) * dk)
                vc = slice(h * dv, (h + 1) * dv)
                cum = cum_ref[b, rows, kc]
                ref = cum[c // 2:c // 2 + 1, :]
                last = cum[c - 1:c, :]
                q = q_ref[b, rows, kc].astype(F32)
                k = k_ref[b, rows, kc].astype(F32)
                v = v_ref[b, rows, vc]
                st = st_ref[b, h]
                inter = _dot_nt((q * jnp.exp(cum)).astype(BF16), st.astype(BF16))
                sc = _dot_nt((q * jnp.exp(cum - ref)).astype(BF16),
                             (k * jnp.exp(ref - cum)).astype(BF16))
                sc = jnp.where(causal, sc, 0.0)
                o = inter + _dot(sc.astype(BF16), v)
                st_ref[b, h] = st * jnp.exp(last) + _dot_tn(v, (k * jnp.exp(last - cum)).astype(BF16))
                o = o * lax.rsqrt(jnp.mean(o * o, -1, keepdims=True) + LN_EPS)
                o_ref[b, rows, vc] = (o * gate_ref[b, rows, vc].astype(F32)).astype(BF16)
        return carry

    lax.fori_loop(0, tb // c, chunk, 0, unroll=True)


def _gla_core(q, k, v, lg, gate, bsz, seq, heads, dk, dv):
    tb = GLA_TILE
    r3 = lambda a: a.reshape(bsz, seq, a.shape[-1])
    ins = [r3(q), r3(k), r3(v), r3(lg), r3(gate)]
    spec = lambda a: pl.BlockSpec((bsz, tb, a.shape[-1]), lambda s: (0, s, 0))
    out = pl.pallas_call(
        functools.partial(_gla_body, heads=heads, dk=dk, dv=dv), grid=(seq // tb,),
        in_specs=[spec(a) for a in ins], out_specs=spec(ins[2]),
        out_shape=jax.ShapeDtypeStruct((bsz, seq, heads * dv), BF16),
        scratch_shapes=[pltpu.VMEM((bsz, tb, heads * dk), F32), pltpu.VMEM((bsz, heads, dv, dk), F32)],
        compiler_params=_params(("arbitrary",)), name="gated_linear_attention")(*ins)
    return out.reshape(bsz * seq, heads * dv)


def _hg_in_body(x_ref, w_ref, lb_ref, q_ref, k_ref, v_ref, lg_ref, gs_ref):
    y = _dot(x_ref[...].astype(BF16), w_ref[...])
    lb = lb_ref[...]
    f = lb + (1.0 - lb) * _sigmoid(y[:, HG_KDIM:2 * HG_KDIM])
    q_ref[...] = _silu(y[:, :HG_KDIM]).astype(BF16)
    k_ref[...] = (1.0 - f).astype(BF16)
    lg_ref[...] = jnp.log(f)
    v_ref[...] = y[:, 2 * HG_KDIM:2 * HG_KDIM + HG_VDIM].astype(BF16)
    gs_ref[...] = _silu(y[:, 2 * HG_KDIM + HG_VDIM:]).astype(BF16)


def _hgrn2_layer(x2d, bsz, seq, w_in, lb, w_out, ln_g, ln_b, w_router):
    outs = [(HG_KDIM, BF16), (HG_KDIM, BF16), (HG_VDIM, BF16), (HG_KDIM, F32), (HG_VDIM, BF16)]
    q, k, v, lg, gs = _row_call(_hg_in_body, [x2d], [w_in.astype(BF16), lb.reshape(1, -1)], outs,
                                "hgrn2_in_proj")
    o = _gla_core(q, k, v, lg, gs, bsz, seq, HG_HEADS, HG_DK, HG_DV)
    return _out_ln_router(o, x2d, w_out, ln_g, ln_b, w_router)


def _gla_in_body(x_ref, w_ref, wl_ref, wg_ref, bg_ref, q_ref, k_ref, v_ref, lg_ref, gs_ref):
    xb = x_ref[...].astype(BF16)
    y = _dot(xb, w_ref[...])
    gl = _dot(xb, wl_ref[...])
    z = _dot(gl.astype(BF16), wg_ref[...]) + bg_ref[...]
    lg_ref[...] = (jnp.minimum(z, 0.0) - jnp.log1p(jnp.exp(-jnp.abs(z)))) * (1.0 / GLA_TAU)
    q_ref[...] = (y[:, :GLA_KDIM] * (GLA_DK ** -0.5)).astype(BF16)
    k_ref[...] = y[:, GLA_KDIM:2 * GLA_KDIM].astype(BF16)
    v_ref[...] = y[:, 2 * GLA_KDIM:2 * GLA_KDIM + GLA_VDIM].astype(BF16)
    gs_ref[...] = _silu(y[:, 2 * GLA_KDIM + GLA_VDIM:]).astype(BF16)


def _gla_layer(x2d, bsz, seq, w_in, w_gate, b_gate, w_out, ln_g, ln_b, w_router):
    n_main = 2 * GLA_KDIM + 2 * GLA_VDIM
    w_main = w_in[:, :n_main].astype(BF16)
    w_low = jnp.pad(w_in[:, n_main:], ((0, 0), (0, LANES - GLA_RANK))).astype(BF16)
    w_gate_p = jnp.pad(w_gate, ((0, LANES - GLA_RANK), (0, 0))).astype(BF16)
    outs = [(GLA_KDIM, BF16), (GLA_KDIM, BF16), (GLA_VDIM, BF16), (GLA_KDIM, F32), (GLA_VDIM, BF16)]
    q, k, v, lg, gs = _row_call(_gla_in_body, [x2d], [w_main, w_low, w_gate_p, b_gate.reshape(1, -1)],
                                outs, "gla_in_proj")
    o = _gla_core(q, k, v, lg, gs, bsz, seq, GLA_HEADS, GLA_DK, GLA_DV)
    return _out_ln_router(o, x2d, w_out, ln_g, ln_b, w_router)


def _ret_in_body(x_ref, cos_ref, sin_ref, w_ref, q_ref, k_ref, v_ref, gs_ref):
    y = _dot(x_ref[...].astype(BF16), w_ref[...])
    cos = cos_ref[...]
    sin = sin_ref[...]
    half = RET_DK // 2

    def rot(base, scale, dst):
        for h in range(RET_HEADS):
            t1 = y[:, base + h * RET_DK:base + h * RET_DK + half]
            t2 = y[:, base + h * RET_DK + half:base + (h + 1) * RET_DK]
            dst[:, h * RET_DK:h * RET_DK + half] = ((t1 * cos - t2 * sin) * scale).astype(BF16)
            dst[:, h * RET_DK + half:(h + 1) * RET_DK] = ((t1 * sin + t2 * cos) * scale).astype(BF16)

    rot(0, 1.0, q_ref)
    rot(RET_KDIM, RET_DK ** -0.5, k_ref)
    v_ref[...] = y[:, 2 * RET_KDIM:2 * RET_KDIM + RET_VDIM].astype(BF16)
    gs_ref[...] = _silu(y[:, 2 * RET_KDIM + RET_VDIM:]).astype(BF16)


def _ret_body(q_ref, k_ref, v_ref, gs_ref, dq_ref, dk_ref, dm_ref, gc_ref, o_ref, st_ref):
    bsz = q_ref.shape[0]

    @pl.when(pl.program_id(0) == 0)
    def _():
        st_ref[...] = jnp.zeros_like(st_ref)

    for b in range(bsz):
        for h in range(RET_HEADS):
            kc = slice(h * RET_DK, (h + 1) * RET_DK)
            vc = slice(h * RET_DV, (h + 1) * RET_DV)
            q = q_ref[b, :, kc]
            k = k_ref[b, :, kc]
            v = v_ref[b, :, vc]
            st = st_ref[b, h]
            inter = _dot(q, st.astype(BF16)) * dq_ref[h]
            sc = _dot_nt(q, k) * dm_ref[h]
            o = inter + _dot(sc.astype(BF16), v)
            kd = (k.astype(F32) * dk_ref[h]).astype(BF16)
            st_ref[b, h] = st * gc_ref[h] + _dot_tn(kd, v)
            mu = jnp.mean(o, -1, keepdims=True)
            dlt = o - mu
            o = dlt * lax.rsqrt(jnp.mean(dlt * dlt, -1, keepdims=True) + LN_EPS)
            o_ref[b, :, vc] = (gs_ref[b, :, vc].astype(F32) * o).astype(BF16)


def _ret_layer(x2d, bsz, seq, w_in, w_out, ln_g, ln_b):
    half = RET_DK // 2
    inv = ROPE_BASE ** (-jnp.arange(0, RET_DK, 2, dtype=F32) / RET_DK)
    ang = jnp.arange(seq, dtype=F32)[:, None] * inv[None, :]
    cos, sin = jnp.cos(ang), jnp.sin(ang)
    tm = ROW_TILE
    ns = seq // tm
    t = x2d.shape[0]
    w = w_in.astype(BF16)
    outs = [(RET_KDIM, BF16), (RET_KDIM, BF16), (RET_VDIM, BF16), (RET_VDIM, BF16)]
    q, k, v, gs = pl.pallas_call(
        _ret_in_body, grid=(t // tm,),
        in_specs=[pl.BlockSpec((tm, D_MODEL), lambda i: (i, 0)),
                  pl.BlockSpec((tm, half), lambda i: (i % ns, 0)),
                  pl.BlockSpec((tm, half), lambda i: (i % ns, 0)),
                  _full_spec(w, 1)],
        out_specs=[pl.BlockSpec((tm, c), lambda i: (i, 0)) for c, _ in outs],
        out_shape=[jax.ShapeDtypeStruct((t, c), dt) for c, dt in outs],
        compiler_params=_params(("arbitrary",)), name="retention_in_proj")(x2d, cos, sin, w)

    c = RET_CHUNK
    pos = jnp.arange(c, dtype=F32)
    lgam = jnp.log1p(-jnp.exp2(-5.0 - jnp.arange(RET_HEADS, dtype=F32)))[:, None]
    decay_q = jnp.exp(lgam * (pos + 1.0))[:, :, None]
    decay_k = jnp.exp(lgam * (c - 1.0 - pos))[:, :, None]
    rel = pos[:, None] - pos[None, :]
    dmat = jnp.where(rel >= 0, jnp.exp(lgam[:, :, None] * jnp.maximum(rel, 0.0)), 0.0)
    gchunk = jnp.broadcast_to(jnp.exp(lgam * c)[:, :, None], (RET_HEADS, 1, LANES))
    decay_q = jnp.broadcast_to(decay_q, (RET_HEADS, c, LANES))
    decay_k = jnp.broadcast_to(decay_k, (RET_HEADS, c, LANES))
    r3 = lambda a: a.reshape(bsz, seq, a.shape[-1])
    ins = [r3(q), r3(k), r3(v), r3(gs)]
    tabs = [decay_q[:, :, :1], decay_k[:, :, :1], dmat, gchunk[:, :, :1]]
    spec = lambda a: pl.BlockSpec((bsz, c, a.shape[-1]), lambda s: (0, s, 0))
    o = pl.pallas_call(
        _ret_body, grid=(seq // c,),
        in_specs=[spec(a) for a in ins] + [_full_spec(a, 1) for a in tabs],
        out_specs=spec(ins[2]), out_shape=jax.ShapeDtypeStruct((bsz, seq, RET_VDIM), BF16),
        scratch_shapes=[pltpu.VMEM((bsz, RET_HEADS, RET_DK, RET_DV), F32)],
        compiler_params=_params(("arbitrary",)), name="retention")(*ins, *tabs)
    return _out_ln(o.reshape(t, RET_VDIM), x2d, w_out, ln_g, ln_b)


def _out_ln_router_body(o_ref, x_ref, w_ref, g_ref, b_ref, wh_ref, wl_ref, y_ref, meta_ref, cnt_ref,
                        carry_ref):
    tm = x_ref.shape[0]

    @pl.when(pl.program_id(0) == 0)
    def _():
        carry_ref[...] = jnp.zeros_like(carry_ref)

    x = _layer_norm(ALPHA * x_ref[...] + _dot(o_ref[...], w_ref[...]), g_ref[...], b_ref[...])
    y_ref[...] = x
    xh = x.astype(BF16)
    xl = (x - xh.astype(F32)).astype(BF16)
    logits = _dot(xh, wh_ref[...]) + (_dot(xl, wh_ref[...]) + _dot(xh, wl_ref[...]))
    lane = lax.broadcasted_iota(I32, (tm, LANES), 1).astype(F32)
    neg = jnp.float32(-jnp.inf)
    big = jnp.float32(LANES)
    m0 = jnp.where(lane < N_EXPERTS, logits, neg)
    v1 = jnp.max(m0, -1, keepdims=True)
    i1 = jnp.min(jnp.where(m0 == v1, lane, big), -1, keepdims=True)
    m1 = jnp.where(lane == i1, neg, m0)
    v2 = jnp.max(m1, -1, keepdims=True)
    i2 = jnp.min(jnp.where(m1 == v2, lane, big), -1, keepdims=True)
    e2 = jnp.exp(v2 - v1)
    w1 = 1.0 / (1.0 + e2)
    w2 = e2 / (1.0 + e2)
    oh1 = (lane == i1).astype(F32)
    oh2 = (lane == i2).astype(F32)
    oh = oh1 + oh2
    ti = lax.broadcasted_iota(I32, (tm, tm), 0)
    si = lax.broadcasted_iota(I32, (tm, tm), 1)
    before = _dot((ti > si).astype(BF16), oh.astype(BF16)) + carry_ref[...]
    r1 = jnp.sum(before * oh1, -1, keepdims=True)
    r2 = jnp.sum(before * oh2, -1, keepdims=True)
    carry_ref[...] += jnp.sum(oh, 0, keepdims=True)
    cnt_ref[...] = jnp.broadcast_to(carry_ref[...], cnt_ref.shape)
    meta = jnp.zeros((tm, LANES), F32)
    for col, val in enumerate((i1, i2, w1, w2, r1, r2)):
        meta = jnp.where(lane == col, val, meta)
    meta_ref[...] = meta


def _out_ln_router(o2d, x2d, w_out, ln_g, ln_b, w_router):
    t = x2d.shape[0]
    rt = ROW_TILE
    wr = jnp.pad(w_router, ((0, 0), (0, LANES - N_EXPERTS)))
    wr_hi = wr.astype(BF16)
    wr_lo = (wr - wr_hi.astype(F32)).astype(BF16)
    fulls = [w_out.astype(BF16), ln_g.reshape(1, -1), ln_b.reshape(1, -1), wr_hi, wr_lo]
    row = lambda c: pl.BlockSpec((rt, c), lambda i: (i, 0))
    return pl.pallas_call(
        _out_ln_router_body, grid=(t // rt,),
        in_specs=[row(o2d.shape[1]), row(D_MODEL)] + [_full_spec(a, 1) for a in fulls],
        out_specs=[row(D_MODEL), row(LANES), pl.BlockSpec((SUBLANES, LANES), lambda i: (0, 0))],
        out_shape=[jax.ShapeDtypeStruct((t, D_MODEL), F32), jax.ShapeDtypeStruct((t, LANES), F32),
                   jax.ShapeDtypeStruct((SUBLANES, LANES), F32)],
        scratch_shapes=[pltpu.VMEM((1, LANES), F32)],
        compiler_params=_params(("arbitrary",)), name="mixer_out_postnorm_router")(o2d, x2d, *fulls)


def _moe_ffn_body(te_ref, nv_ref, src_ref, dst_ref, x_hbm, wg_ref, wu_ref, wo_ref, y_hbm,
                  xbuf, xs, acc, ybuf, gsem, ssem, *, n_tok_rows):
    tm = acc.shape[0]
    i = pl.program_id(0)
    j = pl.program_id(1)
    nf = pl.num_programs(1)
    rs = tm // MOE_NF
    nv = nv_ref[0]
    valid = i < nv
    slot = i % 2
    other = 1 - slot

    def gather_row(buf, r, tok):
        return pltpu.make_async_copy(x_hbm.at[pl.ds(tok, 1), :], xbuf.at[buf, pl.ds(r, 1), :], gsem.at[buf])

    def scatter_row(buf, r, pos):
        return pltpu.make_async_copy(ybuf.at[buf, pl.ds(r, 1), :], y_hbm.at[pl.ds(pos, 1), :], ssem.at[buf])

    def gather_tile(buf):
        return pltpu.make_async_copy(x_hbm.at[pl.ds(0, tm), :], xbuf.at[buf], gsem.at[buf])

    def scatter_tile(buf):
        return pltpu.make_async_copy(ybuf.at[buf], y_hbm.at[pl.ds(0, tm), :], ssem.at[buf])

    @pl.when((i == 0) & (j == 0))
    def _():
        def issue(r, c):
            gather_row(0, r, src_ref[r]).start()
            return c
        lax.fori_loop(0, tm, issue, 0)
        ybuf[1] = jnp.zeros(ybuf.shape[1:], ybuf.dtype)
        n_spare_tiles = (y_hbm.shape[0] - n_tok_rows) // tm
        for k in range(n_spare_tiles):
            pltpu.make_async_copy(ybuf.at[1], y_hbm.at[pl.ds(n_tok_rows + k * tm, tm), :], ssem.at[1]).start()
        for k in range(n_spare_tiles):
            scatter_tile(1).wait()

    @pl.when(valid & (j == 0))
    def _():
        gather_tile(slot).wait()
        xs[...] = xbuf[slot].astype(BF16)

    @pl.when(valid)
    def _():
        for u in range(rs):
            r = j * rs + u
            gather_row(other, r, src_ref[(i + 1) * tm + r]).start()
            scatter_row(other, r, dst_ref[i * tm + r]).start()
        xb = xs[...]
        h = (_silu(_dot(xb, wg_ref[...])) * _dot(xb, wu_ref[...])).astype(BF16)
        part = _dot(h, wo_ref[...])

        @pl.when(j == 0)
        def _():
            acc[...] = part

        @pl.when((j > 0) & (j < nf - 1))
        def _():
            acc[...] += part

        @pl.when(j == nf - 1)
        def _():
            @pl.when(i >= 1)
            def _():
                scatter_tile(slot).wait()
            ybuf[slot] = acc[...] + part

    @pl.when((i == nv) & (j == 0))
    def _():
        gather_tile(slot).wait()
        scatter_tile(slot).wait()

        def issue(r, c):
            scatter_row(other, r, dst_ref[i * tm + r]).start()
            return c
        lax.fori_loop(0, tm, issue, 0)
        scatter_tile(other).wait()


def _row_index_body(d1_ref, d2_ref, out_ref):
    t = d1_ref.shape[0]

    def fill(p, c):
        out_ref[p] = -1
        return c

    def put(tok, c):
        out_ref[d1_ref[tok]] = tok
        out_ref[d2_ref[tok]] = t + tok
        return c

    lax.fori_loop(0, out_ref.shape[0], fill, 0, unroll=8)
    lax.fori_loop(0, t, put, 0, unroll=8)


def _moe_combine_body(x_ref, y1_ref, y2_ref, meta_ref, p_ref, g_ref, b_ref, wp_ref, wgate_ref, o_ref):
    meta = meta_ref[...]
    f = meta[:, 2:3] * y1_ref[...] + meta[:, 3:4] * y2_ref[...]
    x2 = _layer_norm(ALPHA * x_ref[...] + f, g_ref[...], b_ref[...])
    o_ref[...] = _ple(x2, p_ref, wp_ref, wgate_ref)


def _moe_layer(x2d, meta, cnt, p2d, w_in, w_out, layer, ln_g, ln_b, w_proj, w_gate):
    t = x2d.shape[0]
    tm = MOE_TILE
    nf = MOE_NF
    assert nf >= 2
    tf = FFN_EXPERT // nf

    n_tiles = (2 * t + N_EXPERTS * (tm - 1)) // tm + 1
    n_rows = n_tiles * tm
    counts = cnt[0, :N_EXPERTS].astype(I32)
    padded = ((counts + tm - 1) // tm) * tm
    ends = jnp.cumsum(padded)
    offs = ends - padded
    e1, e2 = meta[:, 0].astype(I32), meta[:, 1].astype(I32)
    d1 = offs[e1] + meta[:, 4].astype(I32)
    d2 = offs[e2] + meta[:, 5].astype(I32)
    slot_of_row = pl.pallas_call(
        _row_index_body,
        grid_spec=pltpu.PrefetchScalarGridSpec(
            num_scalar_prefetch=2, grid=(1,), in_specs=[],
            out_specs=pl.BlockSpec(memory_space=pltpu.SMEM)),
        out_shape=jax.ShapeDtypeStruct((n_rows,), I32),
        compiler_params=_params(("arbitrary",)), name="moe_row_index")(d1, d2)
    is_pad = slot_of_row < 0
    src = jnp.where(is_pad, 0, slot_of_row % t)
    spare = 2 * t + tm + jnp.cumsum(is_pad.astype(I32)) - 1
    dst = jnp.where(is_pad, spare, slot_of_row)
    src = jnp.concatenate([src, jnp.zeros((tm,), I32)])
    dst = jnp.concatenate([2 * t + jnp.arange(tm, dtype=I32), dst])
    assert (2 * t) % tm == 0
    n_out = 2 * t + tm + (n_rows - 2 * t)
    n_valid = (ends[-1] // tm).astype(I32)
    starts = jnp.arange(n_tiles, dtype=I32) * tm
    te = jnp.minimum(jnp.sum(ends[None, :] <= starts[:, None], axis=1), N_EXPERTS - 1).astype(I32)
    te = jnp.where(jnp.arange(n_tiles) < n_valid, te, te[jnp.maximum(n_valid - 1, 0)])

    def fj(i, j, nv):
        return jnp.where(i < nv[0], j, nf - 1)

    lyr = layer

    y = pl.pallas_call(
        functools.partial(_moe_ffn_body, n_tok_rows=2 * t),
        grid_spec=pltpu.PrefetchScalarGridSpec(
            num_scalar_prefetch=4, grid=(n_tiles, nf),
            in_specs=[pl.BlockSpec(memory_space=pl.ANY),
                      pl.BlockSpec((None, None, D_MODEL, tf),
                                   lambda i, j, te, nv, s, d: (lyr, te[i], 0, fj(i, j, nv))),
                      pl.BlockSpec((None, None, D_MODEL, tf),
                                   lambda i, j, te, nv, s, d: (lyr, te[i], 0, nf + fj(i, j, nv))),
                      pl.BlockSpec((None, None, tf, D_MODEL),
                                   lambda i, j, te, nv, s, d: (lyr, te[i], fj(i, j, nv), 0))],
            out_specs=pl.BlockSpec(memory_space=pl.ANY),
            scratch_shapes=[pltpu.VMEM((2, tm, D_MODEL), F32), pltpu.VMEM((tm, D_MODEL), BF16),
                            pltpu.VMEM((tm, D_MODEL), F32), pltpu.VMEM((2, tm, D_MODEL), F32),
                            pltpu.SemaphoreType.DMA((2,)), pltpu.SemaphoreType.DMA((2,))]),
        out_shape=jax.ShapeDtypeStruct((n_out, D_MODEL), F32),
        compiler_params=_params(("arbitrary", "arbitrary")), name="moe_expert_ffn",
    )(te, n_valid.reshape(1), src, dst, x2d, w_in, w_in, w_out)

    ct = 256
    nct = t // ct
    fulls = [ln_g.reshape(1, -1), ln_b.reshape(1, -1), w_proj.astype(BF16), w_gate.astype(BF16)]
    row = lambda c: pl.BlockSpec((ct, c), lambda i: (i, 0))
    return pl.pallas_call(
        _moe_combine_body, grid=(nct,),
        in_specs=[row(D_MODEL), row(D_MODEL), pl.BlockSpec((ct, D_MODEL), lambda i: (nct + i, 0)),
                  row(LANES), row(PLE_DIM)] + [_full_spec(a, 1) for a in fulls],
        out_specs=row(D_MODEL), out_shape=jax.ShapeDtypeStruct((t, D_MODEL), F32),
        compiler_params=_params(("arbitrary",)), name="moe_combine_postnorm_ple",
    )(x2d, y, y, meta, p2d, *fulls)


def kernel(x, p, rg_w_in, rg_conv_w, rg_conv_b, rg_w_a, rg_b_a, rg_w_x, rg_b_x, rg_lambda, rg_w_out,
           hg_w_in, hg_lb_logits, hg_w_out, ret_w_in, ret_w_out, gla_w_in, gla_w_gate, gla_b_gate,
           gla_w_out, dense_w_in, dense_w_out, moe_w_router, moe_w_in, moe_w_out, ple_w_proj,
           ple_w_gate, ln_mix_g, ln_mix_b, ln_ffn_g, ln_ffn_b):
    bsz, seq, d = x.shape
    t = bsz * seq
    depth = p.shape[0]
    moe_w_in_bf = moe_w_in.astype(BF16)
    moe_w_out_bf = moe_w_out.astype(BF16)
    lb_sm = jax.nn.softmax(hg_lb_logits.astype(F32), axis=0)
    lower_bounds = jnp.cumsum(lb_sm, axis=0) - lb_sm[0:1]
    h = x.reshape(t, d)
    for i in range(depth):
        kind, j = i % 4, i // 4
        if kind == 0:
            h = _rg_layer(h, bsz, seq, rg_w_in[j], rg_conv_w[j], rg_conv_b[j], rg_w_a[j], rg_b_a[j],
                          rg_w_x[j], rg_b_x[j], rg_lambda[j], rg_w_out[j], ln_mix_g[i], ln_mix_b[i])
        elif kind == 1:
            h, meta, cnt = _hgrn2_layer(h, bsz, seq, hg_w_in[j], lower_bounds[i], hg_w_out[j],
                                        ln_mix_g[i], ln_mix_b[i], moe_w_router[i // 2])
        elif kind == 2:
            h = _ret_layer(h, bsz, seq, ret_w_in[j], ret_w_out[j], ln_mix_g[i], ln_mix_b[i])
        else:
            h, meta, cnt = _gla_layer(h, bsz, seq, gla_w_in[j], gla_w_gate[j], gla_b_gate[j], gla_w_out[j],
                                      ln_mix_g[i], ln_mix_b[i], moe_w_router[i // 2])
        p2d = p[i].reshape(t, p.shape[-1])
        if i % 2 == 0:
            h = _dense_ffn_layer(h, p2d, dense_w_in[i // 2], dense_w_out[i // 2], ln_ffn_g[i], ln_ffn_b[i],
                                 ple_w_proj[i], ple_w_gate[i])
        else:
            h = _moe_layer(h, meta, cnt, p2d, moe_w_in_bf, moe_w_out_bf, i // 2,
                           ln_ffn_g[i], ln_ffn_b[i], ple_w_proj[i], ple_w_gate[i])
    return h.reshape(bsz, seq, d)
```

```python
import functools
import math

import jax
import jax.numpy as jnp
from jax import lax
from jax.experimental import pallas as pl
from jax.experimental.pallas import tpu as pltpu

F32 = jnp.float32
BF16 = jnp.bfloat16
I32 = jnp.int32

D_MODEL = 1024
DEPTH = 4
ALPHA = (2.0 * DEPTH) ** 0.25
LN_EPS = 1e-5
RG_WIDTH = D_MODEL
RG_BLOCK = 256
RG_BLOCKS = RG_WIDTH // RG_BLOCK
RG_CONV = 4
RG_C = 8.0
HG_HEADS = 8
HG_DK = D_MODEL // HG_HEADS
HG_DV = D_MODEL // HG_HEADS
HG_KDIM = HG_HEADS * HG_DK
HG_VDIM = HG_HEADS * HG_DV
RET_HEADS = 4
RET_DK = D_MODEL // RET_HEADS
RET_DV = 2 * D_MODEL // RET_HEADS
RET_KDIM = RET_HEADS * RET_DK
RET_VDIM = RET_HEADS * RET_DV
ROPE_BASE = 10000.0
GLA_HEADS = 4
GLA_DK = D_MODEL // 2 // GLA_HEADS
GLA_DV = D_MODEL // GLA_HEADS
GLA_KDIM = GLA_HEADS * GLA_DK
GLA_VDIM = GLA_HEADS * GLA_DV
GLA_RANK = 16
GLA_TAU = 16.0
GATE_CHUNK = 32
N_EXPERTS = 8
FFN_EXPERT = 3584
PLE_DIM = 256

LANES = 128
SUBLANES = 8
VMEM_LIMIT = 56 << 20

ROW_TILE = 512
RG_TILE = 512
GLA_TILE = 256
RET_CHUNK = 256
MOE_TILE = 512
MOE_NF = 2


def _dot(a, b):
    return jnp.dot(a, b, preferred_element_type=F32)


def _dot_nt(a, b):
    return lax.dot_general(a, b, (((1,), (1,)), ((), ())), preferred_element_type=F32)


def _dot_tn(a, b):
    return lax.dot_general(a, b, (((0,), (0,)), ((), ())), preferred_element_type=F32)


def _sigmoid(v):
    return jax.nn.sigmoid(v)


def _silu(v):
    return v * jax.nn.sigmoid(v)


def _softplus(v):
    return jnp.maximum(v, 0.0) + jnp.log1p(jnp.exp(-jnp.abs(v)))


def _layer_norm(v, g, b):
    mu = jnp.mean(v, -1, keepdims=True)
    d = v - mu
    var = jnp.mean(d * d, -1, keepdims=True)
    return d * lax.rsqrt(var + LN_EPS) * g + b


def _params(sem):
    return pltpu.CompilerParams(dimension_semantics=sem, vmem_limit_bytes=VMEM_LIMIT)


def _full_spec(a, n_grid):
    nd = a.ndim
    if n_grid == 1:
        return pl.BlockSpec(a.shape, lambda i: (0,) * nd)
    return pl.BlockSpec(a.shape, lambda i, j: (0,) * nd)


def _row_call(body, row_ins, full_ins, outs, name, tm=ROW_TILE):
    t = row_ins[0].shape[0]
    assert t % tm == 0
    in_specs = [pl.BlockSpec((tm, a.shape[1]), lambda i: (i, 0)) for a in row_ins]
    in_specs += [_full_spec(a, 1) for a in full_ins]
    out_shape = [jax.ShapeDtypeStruct((t, c), dt) for c, dt in outs]
    out_specs = [pl.BlockSpec((tm, c), lambda i: (i, 0)) for c, dt in outs]
    return pl.pallas_call(
        body, grid=(t // tm,), in_specs=in_specs, out_specs=out_specs, out_shape=out_shape,
        compiler_params=_params(("arbitrary",)), name=name)(*row_ins, *full_ins)


def _rg_body(x_ref, win_ref, cw_ref, cb_ref, wa_ref, ba_ref, wx_ref, bx_ref, lam_ref, wout_ref,
             g_ref, b_ref, o_ref, prev_ref, hc_ref, a_ref, i_ref, h_ref):
    tc = x_ref.shape[0]
    w = RG_WIDTH

    @pl.when(pl.program_id(1) == 0)
    def _():
        prev_ref[...] = jnp.zeros_like(prev_ref)
        hc_ref[...] = jnp.zeros_like(hc_ref)

    x = x_ref[...]
    y = _dot(x.astype(BF16), win_ref[...])
    gate = jax.nn.gelu(y[:, :w], approximate=True)
    rec = y[:, w:]

    ext = jnp.concatenate([prev_ref[...], rec], axis=0)
    cw = cw_ref[...]
    u = cb_ref[...] + cw[0:1, :] * ext[SUBLANES - 3:SUBLANES - 3 + tc, :]
    for j in range(1, RG_CONV):
        off = SUBLANES - (RG_CONV - 1) + j
        u = u + cw[j:j + 1, :] * ext[off:off + tc, :]
    prev_ref[...] = rec[tc - SUBLANES:, :]

    ra, rx = [], []
    for n in range(RG_BLOCKS):
        ub = u[:, n * RG_BLOCK:(n + 1) * RG_BLOCK].astype(BF16)
        ra.append(_dot(ub, wa_ref[n]))
        rx.append(_dot(ub, wx_ref[n]))
    r = _sigmoid(jnp.concatenate(ra, axis=1) + ba_ref[...])
    ig = _sigmoid(jnp.concatenate(rx, axis=1) + bx_ref[...])
    log_a = (-RG_C * _softplus(-lam_ref[...])) * r
    th = jnp.tanh(log_a)
    a_ref[...] = jnp.exp(log_a)
    i_ref[...] = jnp.sqrt(-2.0 * th / (1.0 - th)) * (ig * u)

    row = lax.broadcasted_iota(I32, (SUBLANES, w), 0)

    def group(gi, hc):
        r0 = pl.multiple_of(gi * SUBLANES, SUBLANES)
        a8 = a_ref[pl.ds(r0, SUBLANES), :]
        b8 = i_ref[pl.ds(r0, SUBLANES), :]
        for d in (1, 2, 4):
            m = row >= d
            a_sh = pltpu.roll(a8, d, 0)
            b_sh = pltpu.roll(b8, d, 0)
            b8 = jnp.where(m, a8 * b_sh + b8, b8)
            a8 = jnp.where(m, a8 * a_sh, a8)
        h8 = a8 * hc + b8
        h_ref[pl.ds(r0, SUBLANES), :] = h8
        return h8[SUBLANES - 1:SUBLANES, :]

    hc_ref[...] = lax.fori_loop(0, tc // SUBLANES, group, hc_ref[...])

    hg = (h_ref[...] * gate).astype(BF16)
    mix = _dot(hg, wout_ref[...])
    o_ref[...] = _layer_norm(ALPHA * x + mix, g_ref[...], b_ref[...])


def _rg_layer(x2d, bsz, seq, w_in, conv_w, conv_b, w_a, b_a, w_x, b_x, lam, w_out, ln_g, ln_b):
    tc = RG_TILE
    ns = seq // tc
    row = lambda v: v.reshape(1, -1)
    fulls = [w_in.astype(BF16), conv_w, row(conv_b), w_a.astype(BF16), row(b_a), w_x.astype(BF16),
             row(b_x), row(lam), w_out.astype(BF16), row(ln_g), row(ln_b)]
    tile = pl.BlockSpec((tc, D_MODEL), lambda b, s: (b * ns + s, 0))
    return pl.pallas_call(
        _rg_body, grid=(bsz, ns),
        in_specs=[tile] + [_full_spec(a, 2) for a in fulls],
        out_specs=tile, out_shape=jax.ShapeDtypeStruct(x2d.shape, F32),
        scratch_shapes=[pltpu.VMEM((SUBLANES, RG_WIDTH), F32), pltpu.VMEM((1, RG_WIDTH), F32),
                        pltpu.VMEM((tc, RG_WIDTH), F32), pltpu.VMEM((tc, RG_WIDTH), F32),
                        pltpu.VMEM((tc, RG_WIDTH), F32)],
        compiler_params=_params(("arbitrary", "arbitrary")), name="rglru_mixer")(x2d, *fulls)


def _ple(x2, p_ref, wp_ref, wgate_ref):
    gate = _sigmoid(_dot(x2.astype(BF16), wgate_ref[...]))
    return x2 + _dot(p_ref[...].astype(BF16), wp_ref[...]) * gate


def _dense_ffn_body(x_ref, p_ref, wg_ref, wu_ref, wo_ref, g_ref, b_ref, wp_ref, wgate_ref, o_ref):
    x = x_ref[...]
    xb = x.astype(BF16)
    h = (_silu(_dot(xb, wg_ref[...])) * _dot(xb, wu_ref[...])).astype(BF16)
    x2 = _layer_norm(ALPHA * x + _dot(h, wo_ref[...]), g_ref[...], b_ref[...])
    o_ref[...] = _ple(x2, p_ref, wp_ref, wgate_ref)


def _dense_ffn_layer(x2d, p2d, w_in, w_out, ln_g, ln_b, w_proj, w_gate):
    f = w_out.shape[0]
    w_in = w_in.astype(BF16)
    fulls = [w_in[:, :f], w_in[:, f:], w_out.astype(BF16), ln_g.reshape(1, -1), ln_b.reshape(1, -1),
             w_proj.astype(BF16), w_gate.astype(BF16)]
    return _row_call(_dense_ffn_body, [x2d, p2d], fulls, [(D_MODEL, F32)], "dense_swiglu_ple", tm=256)[0]


def _out_ln_body(o_ref, x_ref, w_ref, g_ref, b_ref, y_ref):
    y_ref[...] = _layer_norm(ALPHA * x_ref[...] + _dot(o_ref[...], w_ref[...]), g_ref[...], b_ref[...])


def _out_ln(o2d, x2d, w_out, ln_g, ln_b):
    fulls = [w_out.astype(BF16), ln_g.reshape(1, -1), ln_b.reshape(1, -1)]
    return _row_call(_out_ln_body, [o2d, x2d], fulls, [(D_MODEL, F32)], "mixer_out_postnorm")[0]


def _gla_body(q_ref, k_ref, v_ref, lg_ref, gate_ref, o_ref, cum_ref, st_ref, *, heads, dk, dv):
    bsz, tb, _ = q_ref.shape
    c = GATE_CHUNK

    @pl.when(pl.program_id(0) == 0)
    def _():
        st_ref[...] = jnp.zeros_like(st_ref)

    rmod = lax.broadcasted_iota(I32, (tb, heads * dk), 0) % c
    for b in range(bsz):
        cum = lg_ref[b]
        d = 1
        while d < c:
            cum = cum + jnp.where(rmod >= d, pltpu.roll(cum, d, 0), 0.0)
            d *= 2
        cum_ref[b] = cum

    ti = lax.broadcasted_iota(I32, (c, c), 0)
    si = lax.broadcasted_iota(I32, (c, c), 1)
    causal = ti >= si

    def chunk(ci, carry):
        r0 = pl.multiple_of(ci * c, c)
        rows = pl.ds(r0, c)
        for b in range(bsz):
            for h in range(heads):
                kc = slice(h * dk, (h + 1) * dk)
                vc = slice(h * dv, (h + 1) * dv)
                cum = cum_ref[b, rows, kc]
                ref = cum[c // 2:c // 2 + 1, :]
                last = cum[c - 1:c, :]
                q = q_ref[b, rows, kc].astype(F32)
                k = k_ref[b, rows, kc].astype(F32)
                v = v_ref[b, rows, vc]
                st = st_ref[b, h]
                inter = _dot_nt((q * jnp.exp(cum)).astype(BF16), st.astype(BF16))
                sc = _dot_nt((q * jnp.exp(cum - ref)).astype(BF16),
                             (k * jnp.exp(ref - cum)).astype(BF16))
                sc = jnp.where(causal, sc, 0.0)
                o = inter + _dot(sc.astype(BF16), v)
                st_ref[b, h] = st * jnp.exp(last) + _dot_tn(v, (k * jnp.exp(last - cum)).astype(BF16))
                o = o * lax.rsqrt(jnp.mean(o * o, -1, keepdims=True) + LN_EPS)
                o_ref[b, rows, vc] = (o * gate_ref[b, rows, vc].astype(F32)).astype(BF16)
        return carry

    lax.fori_loop(0, tb // c, chunk, 0, unroll=True)


def _gla_core(q, k, v, lg, gate, bsz, seq, heads, dk, dv):
    tb = GLA_TILE
    r3 = lambda a: a.reshape(bsz, seq, a.shape[-1])
    ins = [r3(q), r3(k), r3(v), r3(lg), r3(gate)]
    spec = lambda a: pl.BlockSpec((bsz, tb, a.shape[-1]), lambda s: (0, s, 0))
    out = pl.pallas_call(
        functools.partial(_gla_body, heads=heads, dk=dk, dv=dv), grid=(seq // tb,),
        in_specs=[spec(a) for a in ins], out_specs=spec(ins[2]),
        out_shape=jax.ShapeDtypeStruct((bsz, seq, heads * dv), BF16),
        scratch_shapes=[pltpu.VMEM((bsz, tb, heads * dk), F32), pltpu.VMEM((bsz, heads, dv, dk), F32)],
        compiler_params=_params(("arbitrary",)), name="gated_linear_attention")(*ins)
    return out.reshape(bsz * seq, heads * dv)


def _hg_in_body(x_ref, w_ref, lb_ref, q_ref, k_ref, v_ref, lg_ref, gs_ref):
    y = _dot(x_ref[...].astype(BF16), w_ref[...])
    lb = lb_ref[...]
    f = lb + (1.0 - lb) * _sigmoid(y[:, HG_KDIM:2 * HG_KDIM])
    q_ref[...] = _silu(y[:, :HG_KDIM]).astype(BF16)
    k_ref[...] = (1.0 - f).astype(BF16)
    lg_ref[...] = jnp.log(f)
    v_ref[...] = y[:, 2 * HG_KDIM:2 * HG_KDIM + HG_VDIM].astype(BF16)
    gs_ref[...] = _silu(y[:, 2 * HG_KDIM + HG_VDIM:]).astype(BF16)


def _hgrn2_layer(x2d, bsz, seq, w_in, lb, w_out, ln_g, ln_b):
    outs = [(HG_KDIM, BF16), (HG_KDIM, BF16), (HG_VDIM, BF16), (HG_KDIM, F32), (HG_VDIM, BF16)]
    q, k, v, lg, gs = _row_call(_hg_in_body, [x2d], [w_in.astype(BF16), lb.reshape(1, -1)], outs,
                                "hgrn2_in_proj")
    o = _gla_core(q, k, v, lg, gs, bsz, seq, HG_HEADS, HG_DK, HG_DV)
    return _out_ln(o, x2d, w_out, ln_g, ln_b)


def _gla_in_body(x_ref, w_ref, wl_ref, wg_ref, bg_ref, q_ref, k_ref, v_ref, lg_ref, gs_ref):
    xb = x_ref[...].astype(BF16)
    y = _dot(xb, w_ref[...])
    gl = _dot(xb, wl_ref[...])
    z = _dot(gl.astype(BF16), wg_ref[...]) + bg_ref[...]
    lg_ref[...] = (jnp.minimum(z, 0.0) - jnp.log1p(jnp.exp(-jnp.abs(z)))) * (1.0 / GLA_TAU)
    q_ref[...] = (y[:, :GLA_KDIM] * (GLA_DK ** -0.5)).astype(BF16)
    k_ref[...] = y[:, GLA_KDIM:2 * GLA_KDIM].astype(BF16)
    v_ref[...] = y[:, 2 * GLA_KDIM:2 * GLA_KDIM + GLA_VDIM].astype(BF16)
    gs_ref[...] = _silu(y[:, 2 * GLA_KDIM + GLA_VDIM:]).astype(BF16)


def _gla_layer(x2d, bsz, seq, w_in, w_gate, b_gate, w_out, ln_g, ln_b):
    n_main = 2 * GLA_KDIM + 2 * GLA_VDIM
    w_main = w_in[:, :n_main].astype(BF16)
    w_low = jnp.pad(w_in[:, n_main:], ((0, 0), (0, LANES - GLA_RANK))).astype(BF16)
    w_gate_p = jnp.pad(w_gate, ((0, LANES - GLA_RANK), (0, 0))).astype(BF16)
    outs = [(GLA_KDIM, BF16), (GLA_KDIM, BF16), (GLA_VDIM, BF16), (GLA_KDIM, F32), (GLA_VDIM, BF16)]
    q, k, v, lg, gs = _row_call(_gla_in_body, [x2d], [w_main, w_low, w_gate_p, b_gate.reshape(1, -1)],
                                outs, "gla_in_proj")
    o = _gla_core(q, k, v, lg, gs, bsz, seq, GLA_HEADS, GLA_DK, GLA_DV)
    return _out_ln(o, x2d, w_out, ln_g, ln_b)


def _ret_in_body(x_ref, cos_ref, sin_ref, w_ref, q_ref, k_ref, v_ref, gs_ref):
    y = _dot(x_ref[...].astype(BF16), w_ref[...])
    cos = cos_ref[...]
    sin = sin_ref[...]
    half = RET_DK // 2

    def rot(base, scale, dst):
        for h in range(RET_HEADS):
            t1 = y[:, base + h * RET_DK:base + h * RET_DK + half]
            t2 = y[:, base + h * RET_DK + half:base + (h + 1) * RET_DK]
            dst[:, h * RET_DK:h * RET_DK + half] = ((t1 * cos - t2 * sin) * scale).astype(BF16)
            dst[:, h * RET_DK + half:(h + 1) * RET_DK] = ((t1 * sin + t2 * cos) * scale).astype(BF16)

    rot(0, 1.0, q_ref)
    rot(RET_KDIM, RET_DK ** -0.5, k_ref)
    v_ref[...] = y[:, 2 * RET_KDIM:2 * RET_KDIM + RET_VDIM].astype(BF16)
    gs_ref[...] = _silu(y[:, 2 * RET_KDIM + RET_VDIM:]).astype(BF16)


def _ret_body(q_ref, k_ref, v_ref, gs_ref, dq_ref, dk_ref, dm_ref, gc_ref, o_ref, st_ref):
    bsz = q_ref.shape[0]

    @pl.when(pl.program_id(0) == 0)
    def _():
        st_ref[...] = jnp.zeros_like(st_ref)

    for b in range(bsz):
        for h in range(RET_HEADS):
            kc = slice(h * RET_DK, (h + 1) * RET_DK)
            vc = slice(h * RET_DV, (h + 1) * RET_DV)
            q = q_ref[b, :, kc]
            k = k_ref[b, :, kc]
            v = v_ref[b, :, vc]
            st = st_ref[b, h]
            inter = _dot(q, st.astype(BF16)) * dq_ref[h]
            sc = _dot_nt(q, k) * dm_ref[h]
            o = inter + _dot(sc.astype(BF16), v)
            kd = (k.astype(F32) * dk_ref[h]).astype(BF16)
            st_ref[b, h] = st * gc_ref[h] + _dot_tn(kd, v)
            mu = jnp.mean(o, -1, keepdims=True)
            dlt = o - mu
            o = dlt * lax.rsqrt(jnp.mean(dlt * dlt, -1, keepdims=True) + LN_EPS)
            o_ref[b, :, vc] = (gs_ref[b, :, vc].astype(F32) * o).astype(BF16)


def _ret_layer(x2d, bsz, seq, w_in, w_out, ln_g, ln_b):
    half = RET_DK // 2
    inv = ROPE_BASE ** (-jnp.arange(0, RET_DK, 2, dtype=F32) / RET_DK)
    ang = jnp.arange(seq, dtype=F32)[:, None] * inv[None, :]
    cos, sin = jnp.cos(ang), jnp.sin(ang)
    tm = ROW_TILE
    ns = seq // tm
    t = x2d.shape[0]
    w = w_in.astype(BF16)
    outs = [(RET_KDIM, BF16), (RET_KDIM, BF16), (RET_VDIM, BF16), (RET_VDIM, BF16)]
    q, k, v, gs = pl.pallas_call(
        _ret_in_body, grid=(t // tm,),
        in_specs=[pl.BlockSpec((tm, D_MODEL), lambda i: (i, 0)),
                  pl.BlockSpec((tm, half), lambda i: (i % ns, 0)),
                  pl.BlockSpec((tm, half), lambda i: (i % ns, 0)),
                  _full_spec(w, 1)],
        out_specs=[pl.BlockSpec((tm, c), lambda i: (i, 0)) for c, _ in outs],
        out_shape=[jax.ShapeDtypeStruct((t, c), dt) for c, dt in outs],
        compiler_params=_params(("arbitrary",)), name="retention_in_proj")(x2d, cos, sin, w)

    c = RET_CHUNK
    pos = jnp.arange(c, dtype=F32)
    lgam = jnp.log1p(-jnp.exp2(-5.0 - jnp.arange(RET_HEADS, dtype=F32)))[:, None]
    decay_q = jnp.exp(lgam * (pos + 1.0))[:, :, None]
    decay_k = jnp.exp(lgam * (c - 1.0 - pos))[:, :, None]
    rel = pos[:, None] - pos[None, :]
    dmat = jnp.where(rel >= 0, jnp.exp(lgam[:, :, None] * jnp.maximum(rel, 0.0)), 0.0)
    gchunk = jnp.broadcast_to(jnp.exp(lgam * c)[:, :, None], (RET_HEADS, 1, LANES))
    decay_q = jnp.broadcast_to(decay_q, (RET_HEADS, c, LANES))
    decay_k = jnp.broadcast_to(decay_k, (RET_HEADS, c, LANES))
    r3 = lambda a: a.reshape(bsz, seq, a.shape[-1])
    ins = [r3(q), r3(k), r3(v), r3(gs)]
    tabs = [decay_q[:, :, :1], decay_k[:, :, :1], dmat, gchunk[:, :, :1]]
    spec = lambda a: pl.BlockSpec((bsz, c, a.shape[-1]), lambda s: (0, s, 0))
    o = pl.pallas_call(
        _ret_body, grid=(seq // c,),
        in_specs=[spec(a) for a in ins] + [_full_spec(a, 1) for a in tabs],
        out_specs=spec(ins[2]), out_shape=jax.ShapeDtypeStruct((bsz, seq, RET_VDIM), BF16),
        scratch_shapes=[pltpu.VMEM((bsz, RET_HEADS, RET_DK, RET_DV), F32)],
        compiler_params=_params(("arbitrary",)), name="retention")(*ins, *tabs)
    return _out_ln(o.reshape(t, RET_VDIM), x2d, w_out, ln_g, ln_b)


def _router_body(x_ref, wh_ref, wl_ref, meta_ref, cnt_ref, carry_ref):
    tm = x_ref.shape[0]

    @pl.when(pl.program_id(0) == 0)
    def _():
        carry_ref[...] = jnp.zeros_like(carry_ref)

    x = x_ref[...]
    xh = x.astype(BF16)
    xl = (x - xh.astype(F32)).astype(BF16)
    logits = _dot(xh, wh_ref[...]) + (_dot(xl, wh_ref[...]) + _dot(xh, wl_ref[...]))
    lane = lax.broadcasted_iota(I32, (tm, LANES), 1).astype(F32)
    neg = jnp.float32(-jnp.inf)
    big = jnp.float32(LANES)
    m0 = jnp.where(lane < N_EXPERTS, logits, neg)
    v1 = jnp.max(m0, -1, keepdims=True)
    i1 = jnp.min(jnp.where(m0 == v1, lane, big), -1, keepdims=True)
    m1 = jnp.where(lane == i1, neg, m0)
    v2 = jnp.max(m1, -1, keepdims=True)
    i2 = jnp.min(jnp.where(m1 == v2, lane, big), -1, keepdims=True)
    e2 = jnp.exp(v2 - v1)
    w1 = 1.0 / (1.0 + e2)
    w2 = e2 / (1.0 + e2)
    oh1 = (lane == i1).astype(F32)
    oh2 = (lane == i2).astype(F32)
    oh = oh1 + oh2
    ti = lax.broadcasted_iota(I32, (tm, tm), 0)
    si = lax.broadcasted_iota(I32, (tm, tm), 1)
    before = _dot((ti > si).astype(BF16), oh.astype(BF16)) + carry_ref[...]
    r1 = jnp.sum(before * oh1, -1, keepdims=True)
    r2 = jnp.sum(before * oh2, -1, keepdims=True)
    carry_ref[...] += jnp.sum(oh, 0, keepdims=True)
    cnt_ref[...] = jnp.broadcast_to(carry_ref[...], cnt_ref.shape)
    meta = jnp.zeros((tm, LANES), F32)
    for col, val in enumerate((i1, i2, w1, w2, r1, r2)):
        meta = jnp.where(lane == col, val, meta)
    meta_ref[...] = meta


def _moe_ffn_body(te_ref, nv_ref, src_ref, dst_ref, x_hbm, wg_ref, wu_ref, wo_ref, y_hbm,
                  xbuf, xs, acc, ybuf, gsem, ssem, *, n_tok_rows):
    tm = acc.shape[0]
    i = pl.program_id(0)
    j = pl.program_id(1)
    nf = pl.num_programs(1)
    rs = tm // MOE_NF
    nv = nv_ref[0]
    valid = i < nv
    slot = i % 2
    other = 1 - slot

    def gather_row(buf, r, tok):
        return pltpu.make_async_copy(x_hbm.at[pl.ds(tok, 1), :], xbuf.at[buf, pl.ds(r, 1), :], gsem.at[buf])

    def scatter_row(buf, r, pos):
        return pltpu.make_async_copy(ybuf.at[buf, pl.ds(r, 1), :], y_hbm.at[pl.ds(pos, 1), :], ssem.at[buf])

    def gather_tile(buf):
        return pltpu.make_async_copy(x_hbm.at[pl.ds(0, tm), :], xbuf.at[buf], gsem.at[buf])

    def scatter_tile(buf):
        return pltpu.make_async_copy(ybuf.at[buf], y_hbm.at[pl.ds(0, tm), :], ssem.at[buf])

    @pl.when((i == 0) & (j == 0))
    def _():
        def issue(r, c):
            gather_row(0, r, src_ref[r]).start()
            return c
        lax.fori_loop(0, tm, issue, 0)
        ybuf[1] = jnp.zeros(ybuf.shape[1:], ybuf.dtype)
        n_spare_tiles = (y_hbm.shape[0] - n_tok_rows) // tm
        for k in range(n_spare_tiles):
            pltpu.make_async_copy(ybuf.at[1], y_hbm.at[pl.ds(n_tok_rows + k * tm, tm), :], ssem.at[1]).start()
        for k in range(n_spare_tiles):
            scatter_tile(1).wait()

    @pl.when(valid & (j == 0))
    def _():
        gather_tile(slot).wait()
        xs[...] = xbuf[slot].astype(BF16)

    @pl.when(valid)
    def _():
        for u in range(rs):
            r = j * rs + u
            gather_row(other, r, src_ref[(i + 1) * tm + r]).start()
            scatter_row(other, r, dst_ref[i * tm + r]).start()
        xb = xs[...]
        h = (_silu(_dot(xb, wg_ref[...])) * _dot(xb, wu_ref[...])).astype(BF16)
        part = _dot(h, wo_ref[...])

        @pl.when(j == 0)
        def _():
            acc[...] = part

        @pl.when(j > 0)
        def _():
            acc[...] += part

    @pl.when(valid & (j == nf - 1))
    def _():
        @pl.when(i >= 1)
        def _():
            scatter_tile(slot).wait()
        ybuf[slot] = acc[...]

    @pl.when((i == nv) & (j == 0))
    def _():
        gather_tile(slot).wait()
        scatter_tile(slot).wait()

        def issue(r, c):
            scatter_row(other, r, dst_ref[i * tm + r]).start()
            return c
        lax.fori_loop(0, tm, issue, 0)
        scatter_tile(other).wait()


def _row_index_body(d1_ref, d2_ref, out_ref):
    t = d1_ref.shape[0]

    def fill(p, c):
        out_ref[p] = -1
        return c

    def put(tok, c):
        out_ref[d1_ref[tok]] = tok
        out_ref[d2_ref[tok]] = t + tok
        return c

    lax.fori_loop(0, out_ref.shape[0], fill, 0, unroll=8)
    lax.fori_loop(0, t, put, 0, unroll=8)


def _moe_combine_body(x_ref, y1_ref, y2_ref, meta_ref, p_ref, g_ref, b_ref, wp_ref, wgate_ref, o_ref):
    meta = meta_ref[...]
    f = meta[:, 2:3] * y1_ref[...] + meta[:, 3:4] * y2_ref[...]
    x2 = _layer_norm(ALPHA * x_ref[...] + f, g_ref[...], b_ref[...])
    o_ref[...] = _ple(x2, p_ref, wp_ref, wgate_ref)


def _moe_layer(x2d, p2d, w_router, w_in, w_out, layer, ln_g, ln_b, w_proj, w_gate):
    t = x2d.shape[0]
    tm = MOE_TILE
    nf = MOE_NF
    tf = FFN_EXPERT // nf

    wr = jnp.pad(w_router, ((0, 0), (0, LANES - N_EXPERTS)))
    wr_hi = wr.astype(BF16)
    wr_lo = (wr - wr_hi.astype(F32)).astype(BF16)
    rt = ROW_TILE
    meta, cnt = pl.pallas_call(
        _router_body, grid=(t // rt,),
        in_specs=[pl.BlockSpec((rt, D_MODEL), lambda i: (i, 0)), _full_spec(wr_hi, 1), _full_spec(wr_lo, 1)],
        out_specs=[pl.BlockSpec((rt, LANES), lambda i: (i, 0)), pl.BlockSpec((SUBLANES, LANES), lambda i: (0, 0))],
        out_shape=[jax.ShapeDtypeStruct((t, LANES), F32), jax.ShapeDtypeStruct((SUBLANES, LANES), F32)],
        scratch_shapes=[pltpu.VMEM((1, LANES), F32)],
        compiler_params=_params(("arbitrary",)), name="moe_router")(x2d, wr_hi, wr_lo)

    n_tiles = (2 * t + N_EXPERTS * (tm - 1)) // tm + 1
    n_rows = n_tiles * tm
    counts = cnt[0, :N_EXPERTS].astype(I32)
    padded = ((counts + tm - 1) // tm) * tm
    ends = jnp.cumsum(padded)
    offs = ends - padded
    e1, e2 = meta[:, 0].astype(I32), meta[:, 1].astype(I32)
    d1 = offs[e1] + meta[:, 4].astype(I32)
    d2 = offs[e2] + meta[:, 5].astype(I32)
    slot_of_row = pl.pallas_call(
        _row_index_body,
        grid_spec=pltpu.PrefetchScalarGridSpec(
            num_scalar_prefetch=2, grid=(1,), in_specs=[],
            out_specs=pl.BlockSpec(memory_space=pltpu.SMEM)),
        out_shape=jax.ShapeDtypeStruct((n_rows,), I32),
        compiler_params=_params(("arbitrary",)), name="moe_row_index")(d1, d2)
    is_pad = slot_of_row < 0
    src = jnp.where(is_pad, 0, slot_of_row % t)
    spare = 2 * t + tm + jnp.cumsum(is_pad.astype(I32)) - 1
    dst = jnp.where(is_pad, spare, slot_of_row)
    src = jnp.concatenate([src, jnp.zeros((tm,), I32)])
    dst = jnp.concatenate([2 * t + jnp.arange(tm, dtype=I32), dst])
    assert (2 * t) % tm == 0
    n_out = 2 * t + tm + (n_rows - 2 * t)
    n_valid = (ends[-1] // tm).astype(I32)
    starts = jnp.arange(n_tiles, dtype=I32) * tm
    te = jnp.minimum(jnp.sum(ends[None, :] <= starts[:, None], axis=1), N_EXPERTS - 1).astype(I32)
    te = jnp.where(jnp.arange(n_tiles) < n_valid, te, te[jnp.maximum(n_valid - 1, 0)])

    def fj(i, j, nv):
        return jnp.where(i < nv[0], j, nf - 1)

    lyr = layer

    y = pl.pallas_call(
        functools.partial(_moe_ffn_body, n_tok_rows=2 * t),
        grid_spec=pltpu.PrefetchScalarGridSpec(
            num_scalar_prefetch=4, grid=(n_tiles, nf),
            in_specs=[pl.BlockSpec(memory_space=pl.ANY),
                      pl.BlockSpec((None, None, D_MODEL, tf),
                                   lambda i, j, te, nv, s, d: (lyr, te[i], 0, fj(i, j, nv))),
                      pl.BlockSpec((None, None, D_MODEL, tf),
                                   lambda i, j, te, nv, s, d: (lyr, te[i], 0, nf + fj(i, j, nv))),
                      pl.BlockSpec((None, None, tf, D_MODEL),
                                   lambda i, j, te, nv, s, d: (lyr, te[i], fj(i, j, nv), 0))],
            out_specs=pl.BlockSpec(memory_space=pl.ANY),
            scratch_shapes=[pltpu.VMEM((2, tm, D_MODEL), F32), pltpu.VMEM((tm, D_MODEL), BF16),
                            pltpu.VMEM((tm, D_MODEL), F32), pltpu.VMEM((2, tm, D_MODEL), F32),
                            pltpu.SemaphoreType.DMA((2,)), pltpu.SemaphoreType.DMA((2,))]),
        out_shape=jax.ShapeDtypeStruct((n_out, D_MODEL), F32),
        compiler_params=_params(("arbitrary", "arbitrary")), name="moe_expert_ffn",
    )(te, n_valid.reshape(1), src, dst, x2d, w_in, w_in, w_out)

    ct = 256
    nct = t // ct
    fulls = [ln_g.reshape(1, -1), ln_b.reshape(1, -1), w_proj.astype(BF16), w_gate.astype(BF16)]
    row = lambda c: pl.BlockSpec((ct, c), lambda i: (i, 0))
    return pl.pallas_call(
        _moe_combine_body, grid=(nct,),
        in_specs=[row(D_MODEL), row(D_MODEL), pl.BlockSpec((ct, D_MODEL), lambda i: (nct + i, 0)),
                  row(LANES), row(PLE_DIM)] + [_full_spec(a, 1) for a in fulls],
        out_specs=row(D_MODEL), out_shape=jax.ShapeDtypeStruct((t, D_MODEL), F32),
        compiler_params=_params(("arbitrary",)), name="moe_combine_postnorm_ple",
    )(x2d, y, y, meta, p2d, *fulls)


def kernel(x, p, rg_w_in, rg_conv_w, rg_conv_b, rg_w_a, rg_b_a, rg_w_x, rg_b_x, rg_lambda, rg_w_out,
           hg_w_in, hg_lb_logits, hg_w_out, ret_w_in, ret_w_out, gla_w_in, gla_w_gate, gla_b_gate,
           gla_w_out, dense_w_in, dense_w_out, moe_w_router, moe_w_in, moe_w_out, ple_w_proj,
           ple_w_gate, ln_mix_g, ln_mix_b, ln_ffn_g, ln_ffn_b):
    bsz, seq, d = x.shape
    t = bsz * seq
    depth = p.shape[0]
    moe_w_in_bf = moe_w_in.astype(BF16)
    moe_w_out_bf = moe_w_out.astype(BF16)
    lb_sm = jax.nn.softmax(hg_lb_logits.astype(F32), axis=0)
    lower_bounds = jnp.cumsum(lb_sm, axis=0) - lb_sm[0:1]
    h = x.reshape(t, d)
    for i in range(depth):
        kind, j = i % 4, i // 4
        if kind == 0:
            h = _rg_layer(h, bsz, seq, rg_w_in[j], rg_conv_w[j], rg_conv_b[j], rg_w_a[j], rg_b_a[j],
                          rg_w_x[j], rg_b_x[j], rg_lambda[j], rg_w_out[j], ln_mix_g[i], ln_mix_b[i])
        elif kind == 1:
            h = _hgrn2_layer(h, bsz, seq, hg_w_in[j], lower_bounds[i], hg_w_out[j], ln_mix_g[i], ln_mix_b[i])
        elif kind == 2:
            h = _ret_layer(h, bsz, seq, ret_w_in[j], ret_w_out[j], ln_mix_g[i], ln_mix_b[i])
        else:
            h = _gla_layer(h, bsz, seq, gla_w_in[j], gla_w_gate[j], gla_b_gate[j], gla_w_out[j],
                           ln_mix_g[i], ln_mix_b[i])
        p2d = p[i].reshape(t, p.shape[-1])
        if i % 2 == 0:
            h = _dense_ffn_layer(h, p2d, dense_w_in[i // 2], dense_w_out[i // 2], ln_ffn_g[i], ln_ffn_b[i],
                                 ple_w_proj[i], ple_w_gate[i])
        else:
            h = _moe_layer(h, p2d, moe_w_router[i // 2], moe_w_in_bf, moe_w_out_bf, i // 2,
                           ln_ffn_g[i], ln_ffn_b[i], ple_w_proj[i], ple_w_gate[i])
    return h.reshape(bsz, seq, d)
```

```python
import functools
import math

import jax
import jax.numpy as jnp
from jax import lax
from jax.experimental import pallas as pl
from jax.experimental.pallas import tpu as pltpu

F32 = jnp.float32
BF16 = jnp.bfloat16
I32 = jnp.int32

D_MODEL = 1024
DEPTH = 4
ALPHA = (2.0 * DEPTH) ** 0.25
LN_EPS = 1e-5
RG_WIDTH = D_MODEL
RG_BLOCK = 256
RG_BLOCKS = RG_WIDTH // RG_BLOCK
RG_CONV = 4
RG_C = 8.0
HG_HEADS = 8
HG_DK = D_MODEL // HG_HEADS
HG_DV = D_MODEL // HG_HEADS
HG_KDIM = HG_HEADS * HG_DK
HG_VDIM = HG_HEADS * HG_DV
RET_HEADS = 4
RET_DK = D_MODEL // RET_HEADS
RET_DV = 2 * D_MODEL // RET_HEADS
RET_KDIM = RET_HEADS * RET_DK
RET_VDIM = RET_HEADS * RET_DV
ROPE_BASE = 10000.0
GLA_HEADS = 4
GLA_DK = D_MODEL // 2 // GLA_HEADS
GLA_DV = D_MODEL // GLA_HEADS
GLA_KDIM = GLA_HEADS * GLA_DK
GLA_VDIM = GLA_HEADS * GLA_DV
GLA_RANK = 16
GLA_TAU = 16.0
GATE_CHUNK = 32
N_EXPERTS = 8
FFN_EXPERT = 3584
PLE_DIM = 256

LANES = 128
SUBLANES = 8
VMEM_LIMIT = 56 << 20

ROW_TILE = 512
RG_TILE = 512
GLA_TILE = 256
RET_CHUNK = 256
MOE_TILE = 512
MOE_NF = 2


def _dot(a, b):
    return jnp.dot(a, b, preferred_element_type=F32)


def _dot_nt(a, b):
    return lax.dot_general(a, b, (((1,), (1,)), ((), ())), preferred_element_type=F32)


def _dot_tn(a, b):
    return lax.dot_general(a, b, (((0,), (0,)), ((), ())), preferred_element_type=F32)


def _sigmoid(v):
    return jax.nn.sigmoid(v)


def _silu(v):
    return v * jax.nn.sigmoid(v)


def _softplus(v):
    return jnp.maximum(v, 0.0) + jnp.log1p(jnp.exp(-jnp.abs(v)))


def _layer_norm(v, g, b):
    mu = jnp.mean(v, -1, keepdims=True)
    d = v - mu
    var = jnp.mean(d * d, -1, keepdims=True)
    return d * lax.rsqrt(var + LN_EPS) * g + b


def _params(sem):
    return pltpu.CompilerParams(dimension_semantics=sem, vmem_limit_bytes=VMEM_LIMIT)


def _full_spec(a, n_grid):
    nd = a.ndim
    if n_grid == 1:
        return pl.BlockSpec(a.shape, lambda i: (0,) * nd)
    return pl.BlockSpec(a.shape, lambda i, j: (0,) * nd)


def _row_call(body, row_ins, full_ins, outs, name, tm=ROW_TILE):
    t = row_ins[0].shape[0]
    assert t % tm == 0
    in_specs = [pl.BlockSpec((tm, a.shape[1]), lambda i: (i, 0)) for a in row_ins]
    in_specs += [_full_spec(a, 1) for a in full_ins]
    out_shape = [jax.ShapeDtypeStruct((t, c), dt) for c, dt in outs]
    out_specs = [pl.BlockSpec((tm, c), lambda i: (i, 0)) for c, dt in outs]
    return pl.pallas_call(
        body, grid=(t // tm,), in_specs=in_specs, out_specs=out_specs, out_shape=out_shape,
        compiler_params=_params(("arbitrary",)), name=name)(*row_ins, *full_ins)


def _rg_body(x_ref, win_ref, cw_ref, cb_ref, wa_ref, ba_ref, wx_ref, bx_ref, lam_ref, wout_ref,
             g_ref, b_ref, o_ref, prev_ref, hc_ref, a_ref, i_ref, h_ref):
    tc = x_ref.shape[0]
    w = RG_WIDTH

    @pl.when(pl.program_id(1) == 0)
    def _():
        prev_ref[...] = jnp.zeros_like(prev_ref)
        hc_ref[...] = jnp.zeros_like(hc_ref)

    x = x_ref[...]
    y = _dot(x.astype(BF16), win_ref[...])
    gate = jax.nn.gelu(y[:, :w], approximate=True)
    rec = y[:, w:]

    ext = jnp.concatenate([prev_ref[...], rec], axis=0)
    cw = cw_ref[...]
    u = cb_ref[...] + cw[0:1, :] * ext[SUBLANES - 3:SUBLANES - 3 + tc, :]
    for j in range(1, RG_CONV):
        off = SUBLANES - (RG_CONV - 1) + j
        u = u + cw[j:j + 1, :] * ext[off:off + tc, :]
    prev_ref[...] = rec[tc - SUBLANES:, :]

    ra, rx = [], []
    for n in range(RG_BLOCKS):
        ub = u[:, n * RG_BLOCK:(n + 1) * RG_BLOCK].astype(BF16)
        ra.append(_dot(ub, wa_ref[n]))
        rx.append(_dot(ub, wx_ref[n]))
    r = _sigmoid(jnp.concatenate(ra, axis=1) + ba_ref[...])
    ig = _sigmoid(jnp.concatenate(rx, axis=1) + bx_ref[...])
    log_a = (-RG_C * _softplus(-lam_ref[...])) * r
    th = jnp.tanh(log_a)
    a_ref[...] = jnp.exp(log_a)
    i_ref[...] = jnp.sqrt(-2.0 * th / (1.0 - th)) * (ig * u)

    row = lax.broadcasted_iota(I32, (SUBLANES, w), 0)

    def group(gi, hc):
        r0 = pl.multiple_of(gi * SUBLANES, SUBLANES)
        a8 = a_ref[pl.ds(r0, SUBLANES), :]
        b8 = i_ref[pl.ds(r0, SUBLANES), :]
        for d in (1, 2, 4):
            m = row >= d
            a_sh = pltpu.roll(a8, d, 0)
            b_sh = pltpu.roll(b8, d, 0)
            b8 = jnp.where(m, a8 * b_sh + b8, b8)
            a8 = jnp.where(m, a8 * a_sh, a8)
        h8 = a8 * hc + b8
        h_ref[pl.ds(r0, SUBLANES), :] = h8
        return h8[SUBLANES - 1:SUBLANES, :]

    hc_ref[...] = lax.fori_loop(0, tc // SUBLANES, group, hc_ref[...])

    hg = (h_ref[...] * gate).astype(BF16)
    mix = _dot(hg, wout_ref[...])
    o_ref[...] = _layer_norm(ALPHA * x + mix, g_ref[...], b_ref[...])


def _rg_layer(x2d, bsz, seq, w_in, conv_w, conv_b, w_a, b_a, w_x, b_x, lam, w_out, ln_g, ln_b):
    tc = RG_TILE
    ns = seq // tc
    row = lambda v: v.reshape(1, -1)
    fulls = [w_in.astype(BF16), conv_w, row(conv_b), w_a.astype(BF16), row(b_a), w_x.astype(BF16),
             row(b_x), row(lam), w_out.astype(BF16), row(ln_g), row(ln_b)]
    tile = pl.BlockSpec((tc, D_MODEL), lambda b, s: (b * ns + s, 0))
    return pl.pallas_call(
        _rg_body, grid=(bsz, ns),
        in_specs=[tile] + [_full_spec(a, 2) for a in fulls],
        out_specs=tile, out_shape=jax.ShapeDtypeStruct(x2d.shape, F32),
        scratch_shapes=[pltpu.VMEM((SUBLANES, RG_WIDTH), F32), pltpu.VMEM((1, RG_WIDTH), F32),
                        pltpu.VMEM((tc, RG_WIDTH), F32), pltpu.VMEM((tc, RG_WIDTH), F32),
                        pltpu.VMEM((tc, RG_WIDTH), F32)],
        compiler_params=_params(("arbitrary", "arbitrary")), name="rglru_mixer")(x2d, *fulls)


def _ple(x2, p_ref, wp_ref, wgate_ref):
    gate = _sigmoid(_dot(x2.astype(BF16), wgate_ref[...]))
    return x2 + _dot(p_ref[...].astype(BF16), wp_ref[...]) * gate


def _dense_ffn_body(x_ref, p_ref, wg_ref, wu_ref, wo_ref, g_ref, b_ref, wp_ref, wgate_ref, o_ref):
    x = x_ref[...]
    xb = x.astype(BF16)
    h = (_silu(_dot(xb, wg_ref[...])) * _dot(xb, wu_ref[...])).astype(BF16)
    x2 = _layer_norm(ALPHA * x + _dot(h, wo_ref[...]), g_ref[...], b_ref[...])
    o_ref[...] = _ple(x2, p_ref, wp_ref, wgate_ref)


def _dense_ffn_layer(x2d, p2d, w_in, w_out, ln_g, ln_b, w_proj, w_gate):
    f = w_out.shape[0]
    w_in = w_in.astype(BF16)
    fulls = [w_in[:, :f], w_in[:, f:], w_out.astype(BF16), ln_g.reshape(1, -1), ln_b.reshape(1, -1),
             w_proj.astype(BF16), w_gate.astype(BF16)]
    return _row_call(_dense_ffn_body, [x2d, p2d], fulls, [(D_MODEL, F32)], "dense_swiglu_ple", tm=256)[0]


def _out_ln_body(o_ref, x_ref, w_ref, g_ref, b_ref, y_ref):
    y_ref[...] = _layer_norm(ALPHA * x_ref[...] + _dot(o_ref[...], w_ref[...]), g_ref[...], b_ref[...])


def _out_ln(o2d, x2d, w_out, ln_g, ln_b):
    fulls = [w_out.astype(BF16), ln_g.reshape(1, -1), ln_b.reshape(1, -1)]
    return _row_call(_out_ln_body, [o2d, x2d], fulls, [(D_MODEL, F32)], "mixer_out_postnorm")[0]


def _gla_body(q_ref, k_ref, v_ref, lg_ref, gate_ref, o_ref, cum_ref, st_ref, *, heads, dk, dv):
    bsz, tb, _ = q_ref.shape
    c = GATE_CHUNK

    @pl.when(pl.program_id(0) == 0)
    def _():
        st_ref[...] = jnp.zeros_like(st_ref)

    rmod = lax.broadcasted_iota(I32, (tb, heads * dk), 0) % c
    for b in range(bsz):
        cum = lg_ref[b]
        d = 1
        while d < c:
            cum = cum + jnp.where(rmod >= d, pltpu.roll(cum, d, 0), 0.0)
            d *= 2
        cum_ref[b] = cum

    ti = lax.broadcasted_iota(I32, (c, c), 0)
    si = lax.broadcasted_iota(I32, (c, c), 1)
    causal = ti >= si

    def chunk(ci, carry):
        r0 = pl.multiple_of(ci * c, c)
        rows = pl.ds(r0, c)
        for b in range(bsz):
            for h in range(heads):
                kc = slice(h * dk, (h + 1) * dk)
                vc = slice(h * dv, (h + 1) * dv)
                cum = cum_ref[b, rows, kc]
                ref = cum[c // 2:c // 2 + 1, :]
                last = cum[c - 1:c, :]
                q = q_ref[b, rows, kc].astype(F32)
                k = k_ref[b, rows, kc].astype(F32)
                v = v_ref[b, rows, vc]
                st = st_ref[b, h]
                inter = _dot_nt((q * jnp.exp(cum)).astype(BF16), st.astype(BF16))
                sc = _dot_nt((q * jnp.exp(cum - ref)).astype(BF16),
                             (k * jnp.exp(ref - cum)).astype(BF16))
                sc = jnp.where(causal, sc, 0.0)
                o = inter + _dot(sc.astype(BF16), v)
                st_ref[b, h] = st * jnp.exp(last) + _dot_tn(v, (k * jnp.exp(last - cum)).astype(BF16))
                o = o * lax.rsqrt(jnp.mean(o * o, -1, keepdims=True) + LN_EPS)
                o_ref[b, rows, vc] = (o * gate_ref[b, rows, vc].astype(F32)).astype(BF16)
        return carry

    lax.fori_loop(0, tb // c, chunk, 0, unroll=True)


def _gla_core(q, k, v, lg, gate, bsz, seq, heads, dk, dv):
    tb = GLA_TILE
    r3 = lambda a: a.reshape(bsz, seq, a.shape[-1])
    ins = [r3(q), r3(k), r3(v), r3(lg), r3(gate)]
    spec = lambda a: pl.BlockSpec((bsz, tb, a.shape[-1]), lambda s: (0, s, 0))
    out = pl.pallas_call(
        functools.partial(_gla_body, heads=heads, dk=dk, dv=dv), grid=(seq // tb,),
        in_specs=[spec(a) for a in ins], out_specs=spec(ins[2]),
        out_shape=jax.ShapeDtypeStruct((bsz, seq, heads * dv), BF16),
        scratch_shapes=[pltpu.VMEM((bsz, tb, heads * dk), F32), pltpu.VMEM((bsz, heads, dv, dk), F32)],
        compiler_params=_params(("arbitrary",)), name="gated_linear_attention")(*ins)
    return out.reshape(bsz * seq, heads * dv)


def _hg_in_body(x_ref, w_ref, lb_ref, q_ref, k_ref, v_ref, lg_ref, gs_ref):
    y = _dot(x_ref[...].astype(BF16), w_ref[...])
    lb = lb_ref[...]
    f = lb + (1.0 - lb) * _sigmoid(y[:, HG_KDIM:2 * HG_KDIM])
    q_ref[...] = _silu(y[:, :HG_KDIM]).astype(BF16)
    k_ref[...] = (1.0 - f).astype(BF16)
    lg_ref[...] = jnp.log(f)
    v_ref[...] = y[:, 2 * HG_KDIM:2 * HG_KDIM + HG_VDIM].astype(BF16)
    gs_ref[...] = _silu(y[:, 2 * HG_KDIM + HG_VDIM:]).astype(BF16)


def _hgrn2_layer(x2d, bsz, seq, w_in, lb, w_out, ln_g, ln_b):
    outs = [(HG_KDIM, BF16), (HG_KDIM, BF16), (HG_VDIM, BF16), (HG_KDIM, F32), (HG_VDIM, BF16)]
    q, k, v, lg, gs = _row_call(_hg_in_body, [x2d], [w_in.astype(BF16), lb.reshape(1, -1)], outs,
                                "hgrn2_in_proj")
    o = _gla_core(q, k, v, lg, gs, bsz, seq, HG_HEADS, HG_DK, HG_DV)
    return _out_ln(o, x2d, w_out, ln_g, ln_b)


def _gla_in_body(x_ref, w_ref, wl_ref, wg_ref, bg_ref, q_ref, k_ref, v_ref, lg_ref, gs_ref):
    xb = x_ref[...].astype(BF16)
    y = _dot(xb, w_ref[...])
    gl = _dot(xb, wl_ref[...])
    z = _dot(gl.astype(BF16), wg_ref[...]) + bg_ref[...]
    lg_ref[...] = (jnp.minimum(z, 0.0) - jnp.log1p(jnp.exp(-jnp.abs(z)))) * (1.0 / GLA_TAU)
    q_ref[...] = (y[:, :GLA_KDIM] * (GLA_DK ** -0.5)).astype(BF16)
    k_ref[...] = y[:, GLA_KDIM:2 * GLA_KDIM].astype(BF16)
    v_ref[...] = y[:, 2 * GLA_KDIM:2 * GLA_KDIM + GLA_VDIM].astype(BF16)
    gs_ref[...] = _silu(y[:, 2 * GLA_KDIM + GLA_VDIM:]).astype(BF16)


def _gla_layer(x2d, bsz, seq, w_in, w_gate, b_gate, w_out, ln_g, ln_b):
    n_main = 2 * GLA_KDIM + 2 * GLA_VDIM
    w_main = w_in[:, :n_main].astype(BF16)
    w_low = jnp.pad(w_in[:, n_main:], ((0, 0), (0, LANES - GLA_RANK))).astype(BF16)
    w_gate_p = jnp.pad(w_gate, ((0, LANES - GLA_RANK), (0, 0))).astype(BF16)
    outs = [(GLA_KDIM, BF16), (GLA_KDIM, BF16), (GLA_VDIM, BF16), (GLA_KDIM, F32), (GLA_VDIM, BF16)]
    q, k, v, lg, gs = _row_call(_gla_in_body, [x2d], [w_main, w_low, w_gate_p, b_gate.reshape(1, -1)],
                                outs, "gla_in_proj")
    o = _gla_core(q, k, v, lg, gs, bsz, seq, GLA_HEADS, GLA_DK, GLA_DV)
    return _out_ln(o, x2d, w_out, ln_g, ln_b)


def _ret_in_body(x_ref, cos_ref, sin_ref, w_ref, q_ref, k_ref, v_ref, gs_ref):
    y = _dot(x_ref[...].astype(BF16), w_ref[...])
    cos = cos_ref[...]
    sin = sin_ref[...]
    half = RET_DK // 2

    def rot(base, scale, dst):
        for h in range(RET_HEADS):
            t1 = y[:, base + h * RET_DK:base + h * RET_DK + half]
            t2 = y[:, base + h * RET_DK + half:base + (h + 1) * RET_DK]
            dst[:, h * RET_DK:h * RET_DK + half] = ((t1 * cos - t2 * sin) * scale).astype(BF16)
            dst[:, h * RET_DK + half:(h + 1) * RET_DK] = ((t1 * sin + t2 * cos) * scale).astype(BF16)

    rot(0, 1.0, q_ref)
    rot(RET_KDIM, RET_DK ** -0.5, k_ref)
    v_ref[...] = y[:, 2 * RET_KDIM:2 * RET_KDIM + RET_VDIM].astype(BF16)
    gs_ref[...] = _silu(y[:, 2 * RET_KDIM + RET_VDIM:]).astype(BF16)


def _ret_body(q_ref, k_ref, v_ref, gs_ref, dq_ref, dk_ref, dm_ref, gc_ref, o_ref, st_ref):
    bsz = q_ref.shape[0]

    @pl.when(pl.program_id(0) == 0)
    def _():
        st_ref[...] = jnp.zeros_like(st_ref)

    for b in range(bsz):
        for h in range(RET_HEADS):
            kc = slice(h * RET_DK, (h + 1) * RET_DK)
            vc = slice(h * RET_DV, (h + 1) * RET_DV)
            q = q_ref[b, :, kc]
            k = k_ref[b, :, kc]
            v = v_ref[b, :, vc]
            st = st_ref[b, h]
            inter = _dot(q, st.astype(BF16)) * dq_ref[h]
            sc = _dot_nt(q, k) * dm_ref[h]
            o = inter + _dot(sc.astype(BF16), v)
            kd = (k.astype(F32) * dk_ref[h]).astype(BF16)
            st_ref[b, h] = st * gc_ref[h] + _dot_tn(kd, v)
            mu = jnp.mean(o, -1, keepdims=True)
            dlt = o - mu
            o = dlt * lax.rsqrt(jnp.mean(dlt * dlt, -1, keepdims=True) + LN_EPS)
            o_ref[b, :, vc] = (gs_ref[b, :, vc].astype(F32) * o).astype(BF16)


def _ret_layer(x2d, bsz, seq, w_in, w_out, ln_g, ln_b):
    half = RET_DK // 2
    inv = ROPE_BASE ** (-jnp.arange(0, RET_DK, 2, dtype=F32) / RET_DK)
    ang = jnp.arange(seq, dtype=F32)[:, None] * inv[None, :]
    cos, sin = jnp.cos(ang), jnp.sin(ang)
    tm = ROW_TILE
    ns = seq // tm
    t = x2d.shape[0]
    w = w_in.astype(BF16)
    outs = [(RET_KDIM, BF16), (RET_KDIM, BF16), (RET_VDIM, BF16), (RET_VDIM, BF16)]
    q, k, v, gs = pl.pallas_call(
        _ret_in_body, grid=(t // tm,),
        in_specs=[pl.BlockSpec((tm, D_MODEL), lambda i: (i, 0)),
                  pl.BlockSpec((tm, half), lambda i: (i % ns, 0)),
                  pl.BlockSpec((tm, half), lambda i: (i % ns, 0)),
                  _full_spec(w, 1)],
        out_specs=[pl.BlockSpec((tm, c), lambda i: (i, 0)) for c, _ in outs],
        out_shape=[jax.ShapeDtypeStruct((t, c), dt) for c, dt in outs],
        compiler_params=_params(("arbitrary",)), name="retention_in_proj")(x2d, cos, sin, w)

    c = RET_CHUNK
    pos = jnp.arange(c, dtype=F32)
    lgam = jnp.log1p(-jnp.exp2(-5.0 - jnp.arange(RET_HEADS, dtype=F32)))[:, None]
    decay_q = jnp.exp(lgam * (pos + 1.0))[:, :, None]
    decay_k = jnp.exp(lgam * (c - 1.0 - pos))[:, :, None]
    rel = pos[:, None] - pos[None, :]
    dmat = jnp.where(rel >= 0, jnp.exp(lgam[:, :, None] * jnp.maximum(rel, 0.0)), 0.0)
    gchunk = jnp.broadcast_to(jnp.exp(lgam * c)[:, :, None], (RET_HEADS, 1, LANES))
    decay_q = jnp.broadcast_to(decay_q, (RET_HEADS, c, LANES))
    decay_k = jnp.broadcast_to(decay_k, (RET_HEADS, c, LANES))
    r3 = lambda a: a.reshape(bsz, seq, a.shape[-1])
    ins = [r3(q), r3(k), r3(v), r3(gs)]
    tabs = [decay_q[:, :, :1], decay_k[:, :, :1], dmat, gchunk[:, :, :1]]
    spec = lambda a: pl.BlockSpec((bsz, c, a.shape[-1]), lambda s: (0, s, 0))
    o = pl.pallas_call(
        _ret_body, grid=(seq // c,),
        in_specs=[spec(a) for a in ins] + [_full_spec(a, 1) for a in tabs],
        out_specs=spec(ins[2]), out_shape=jax.ShapeDtypeStruct((bsz, seq, RET_VDIM), BF16),
        scratch_shapes=[pltpu.VMEM((bsz, RET_HEADS, RET_DK, RET_DV), F32)],
        compiler_params=_params(("arbitrary",)), name="retention")(*ins, *tabs)
    return _out_ln(o.reshape(t, RET_VDIM), x2d, w_out, ln_g, ln_b)


def _router_body(x_ref, wh_ref, wl_ref, meta_ref, cnt_ref, carry_ref):
    tm = x_ref.shape[0]

    @pl.when(pl.program_id(0) == 0)
    def _():
        carry_ref[...] = jnp.zeros_like(carry_ref)

    x = x_ref[...]
    xh = x.astype(BF16)
    xl = (x - xh.astype(F32)).astype(BF16)
    logits = _dot(xh, wh_ref[...]) + (_dot(xl, wh_ref[...]) + _dot(xh, wl_ref[...]))
    lane = lax.broadcasted_iota(I32, (tm, LANES), 1).astype(F32)
    neg = jnp.float32(-jnp.inf)
    big = jnp.float32(LANES)
    m0 = jnp.where(lane < N_EXPERTS, logits, neg)
    v1 = jnp.max(m0, -1, keepdims=True)
    i1 = jnp.min(jnp.where(m0 == v1, lane, big), -1, keepdims=True)
    m1 = jnp.where(lane == i1, neg, m0)
    v2 = jnp.max(m1, -1, keepdims=True)
    i2 = jnp.min(jnp.where(m1 == v2, lane, big), -1, keepdims=True)
    e2 = jnp.exp(v2 - v1)
    w1 = 1.0 / (1.0 + e2)
    w2 = e2 / (1.0 + e2)
    oh1 = (lane == i1).astype(F32)
    oh2 = (lane == i2).astype(F32)
    oh = oh1 + oh2
    ti = lax.broadcasted_iota(I32, (tm, tm), 0)
    si = lax.broadcasted_iota(I32, (tm, tm), 1)
    before = _dot((ti > si).astype(BF16), oh.astype(BF16)) + carry_ref[...]
    r1 = jnp.sum(before * oh1, -1, keepdims=True)
    r2 = jnp.sum(before * oh2, -1, keepdims=True)
    carry_ref[...] += jnp.sum(oh, 0, keepdims=True)
    cnt_ref[...] = jnp.broadcast_to(carry_ref[...], cnt_ref.shape)
    meta = jnp.zeros((tm, LANES), F32)
    for col, val in enumerate((i1, i2, w1, w2, r1, r2)):
        meta = jnp.where(lane == col, val, meta)
    meta_ref[...] = meta


def _moe_ffn_body(te_ref, nv_ref, src_ref, dst_ref, x_hbm, wg_ref, wu_ref, wo_ref, y_hbm,
                  xbuf, xs, acc, ybuf, gsem, ssem, *, n_tok_rows):
    tm = acc.shape[0]
    i = pl.program_id(0)
    j = pl.program_id(1)
    nf = pl.num_programs(1)
    rs = tm // MOE_NF
    nv = nv_ref[0]
    valid = i < nv
    slot = i % 2
    other = 1 - slot

    def gather_row(buf, r, tok):
        return pltpu.make_async_copy(x_hbm.at[pl.ds(tok, 1), :], xbuf.at[buf, pl.ds(r, 1), :], gsem.at[buf])

    def scatter_row(buf, r, pos):
        return pltpu.make_async_copy(ybuf.at[buf, pl.ds(r, 1), :], y_hbm.at[pl.ds(pos, 1), :], ssem.at[buf])

    def gather_tile(buf):
        return pltpu.make_async_copy(x_hbm.at[pl.ds(0, tm), :], xbuf.at[buf], gsem.at[buf])

    def scatter_tile(buf):
        return pltpu.make_async_copy(ybuf.at[buf], y_hbm.at[pl.ds(0, tm), :], ssem.at[buf])

    @pl.when((i == 0) & (j == 0))
    def _():
        def issue(r, c):
            gather_row(0, r, src_ref[r]).start()
            return c
        lax.fori_loop(0, tm, issue, 0)
        ybuf[1] = jnp.zeros(ybuf.shape[1:], ybuf.dtype)
        n_spare_tiles = (y_hbm.shape[0] - n_tok_rows) // tm
        for k in range(n_spare_tiles):
            pltpu.make_async_copy(ybuf.at[1], y_hbm.at[pl.ds(n_tok_rows + k * tm, tm), :], ssem.at[1]).start()
        for k in range(n_spare_tiles):
            scatter_tile(1).wait()

    @pl.when(valid & (j == 0))
    def _():
        gather_tile(slot).wait()
        xs[...] = xbuf[slot].astype(BF16)

    @pl.when(valid)
    def _():
        for u in range(rs):
            r = j * rs + u
            gather_row(other, r, src_ref[(i + 1) * tm + r]).start()
            scatter_row(other, r, dst_ref[i * tm + r]).start(priority=u % 2)
        xb = xs[...]
        h = (_silu(_dot(xb, wg_ref[...])) * _dot(xb, wu_ref[...])).astype(BF16)
        part = _dot(h, wo_ref[...])

        @pl.when(j == 0)
        def _():
            acc[...] = part

        @pl.when(j > 0)
        def _():
            acc[...] += part

    @pl.when(valid & (j == nf - 1))
    def _():
        @pl.when(i >= 1)
        def _():
            scatter_tile(slot).wait()
        ybuf[slot] = acc[...]

    @pl.when((i == nv) & (j == 0))
    def _():
        gather_tile(slot).wait()
        scatter_tile(slot).wait()

        def issue(r, c):
            scatter_row(other, r, dst_ref[i * tm + r]).start()
            return c
        lax.fori_loop(0, tm, issue, 0)
        scatter_tile(other).wait()


def _row_index_body(d1_ref, d2_ref, out_ref):
    t = d1_ref.shape[0]

    def fill(p, c):
        out_ref[p] = -1
        return c

    def put(tok, c):
        out_ref[d1_ref[tok]] = tok
        out_ref[d2_ref[tok]] = t + tok
        return c

    lax.fori_loop(0, out_ref.shape[0], fill, 0, unroll=8)
    lax.fori_loop(0, t, put, 0, unroll=8)


def _moe_combine_body(x_ref, y1_ref, y2_ref, meta_ref, p_ref, g_ref, b_ref, wp_ref, wgate_ref, o_ref):
    meta = meta_ref[...]
    f = meta[:, 2:3] * y1_ref[...] + meta[:, 3:4] * y2_ref[...]
    x2 = _layer_norm(ALPHA * x_ref[...] + f, g_ref[...], b_ref[...])
    o_ref[...] = _ple(x2, p_ref, wp_ref, wgate_ref)


def _moe_layer(x2d, p2d, w_router, w_in, w_out, layer, ln_g, ln_b, w_proj, w_gate):
    t = x2d.shape[0]
    tm = MOE_TILE
    nf = MOE_NF
    tf = FFN_EXPERT // nf

    wr = jnp.pad(w_router, ((0, 0), (0, LANES - N_EXPERTS)))
    wr_hi = wr.astype(BF16)
    wr_lo = (wr - wr_hi.astype(F32)).astype(BF16)
    rt = ROW_TILE
    meta, cnt = pl.pallas_call(
        _router_body, grid=(t // rt,),
        in_specs=[pl.BlockSpec((rt, D_MODEL), lambda i: (i, 0)), _full_spec(wr_hi, 1), _full_spec(wr_lo, 1)],
        out_specs=[pl.BlockSpec((rt, LANES), lambda i: (i, 0)), pl.BlockSpec((SUBLANES, LANES), lambda i: (0, 0))],
        out_shape=[jax.ShapeDtypeStruct((t, LANES), F32), jax.ShapeDtypeStruct((SUBLANES, LANES), F32)],
        scratch_shapes=[pltpu.VMEM((1, LANES), F32)],
        compiler_params=_params(("arbitrary",)), name="moe_router")(x2d, wr_hi, wr_lo)

    n_tiles = (2 * t + N_EXPERTS * (tm - 1)) // tm + 1
    n_rows = n_tiles * tm
    counts = cnt[0, :N_EXPERTS].astype(I32)
    padded = ((counts + tm - 1) // tm) * tm
    ends = jnp.cumsum(padded)
    offs = ends - padded
    e1, e2 = meta[:, 0].astype(I32), meta[:, 1].astype(I32)
    d1 = offs[e1] + meta[:, 4].astype(I32)
    d2 = offs[e2] + meta[:, 5].astype(I32)
    slot_of_row = pl.pallas_call(
        _row_index_body,
        grid_spec=pltpu.PrefetchScalarGridSpec(
            num_scalar_prefetch=2, grid=(1,), in_specs=[],
            out_specs=pl.BlockSpec(memory_space=pltpu.SMEM)),
        out_shape=jax.ShapeDtypeStruct((n_rows,), I32),
        compiler_params=_params(("arbitrary",)), name="moe_row_index")(d1, d2)
    is_pad = slot_of_row < 0
    src = jnp.where(is_pad, 0, slot_of_row % t)
    spare = 2 * t + tm + jnp.cumsum(is_pad.astype(I32)) - 1
    dst = jnp.where(is_pad, spare, slot_of_row)
    src = jnp.concatenate([src, jnp.zeros((tm,), I32)])
    dst = jnp.concatenate([2 * t + jnp.arange(tm, dtype=I32), dst])
    assert (2 * t) % tm == 0
    n_out = 2 * t + tm + (n_rows - 2 * t)
    n_valid = (ends[-1] // tm).astype(I32)
    starts = jnp.arange(n_tiles, dtype=I32) * tm
    te = jnp.minimum(jnp.sum(ends[None, :] <= starts[:, None], axis=1), N_EXPERTS - 1).astype(I32)
    te = jnp.where(jnp.arange(n_tiles) < n_valid, te, te[jnp.maximum(n_valid - 1, 0)])

    def fj(i, j, nv):
        return jnp.where(i < nv[0], j, nf - 1)

    lyr = layer

    y = pl.pallas_call(
        functools.partial(_moe_ffn_body, n_tok_rows=2 * t),
        grid_spec=pltpu.PrefetchScalarGridSpec(
            num_scalar_prefetch=4, grid=(n_tiles, nf),
            in_specs=[pl.BlockSpec(memory_space=pl.ANY),
                      pl.BlockSpec((None, None, D_MODEL, tf),
                                   lambda i, j, te, nv, s, d: (lyr, te[i], 0, fj(i, j, nv))),
                      pl.BlockSpec((None, None, D_MODEL, tf),
                                   lambda i, j, te, nv, s, d: (lyr, te[i], 0, nf + fj(i, j, nv))),
                      pl.BlockSpec((None, None, tf, D_MODEL),
                                   lambda i, j, te, nv, s, d: (lyr, te[i], fj(i, j, nv), 0))],
            out_specs=pl.BlockSpec(memory_space=pl.ANY),
            scratch_shapes=[pltpu.VMEM((2, tm, D_MODEL), F32), pltpu.VMEM((tm, D_MODEL), BF16),
                            pltpu.VMEM((tm, D_MODEL), F32), pltpu.VMEM((2, tm, D_MODEL), F32),
                            pltpu.SemaphoreType.DMA((2,)), pltpu.SemaphoreType.DMA((2,))]),
        out_shape=jax.ShapeDtypeStruct((n_out, D_MODEL), F32),
        compiler_params=_params(("arbitrary", "arbitrary")), name="moe_expert_ffn",
    )(te, n_valid.reshape(1), src, dst, x2d, w_in, w_in, w_out)

    ct = 256
    nct = t // ct
    fulls = [ln_g.reshape(1, -1), ln_b.reshape(1, -1), w_proj.astype(BF16), w_gate.astype(BF16)]
    row = lambda c: pl.BlockSpec((ct, c), lambda i: (i, 0))
    return pl.pallas_call(
        _moe_combine_body, grid=(nct,),
        in_specs=[row(D_MODEL), row(D_MODEL), pl.BlockSpec((ct, D_MODEL), lambda i: (nct + i, 0)),
                  row(LANES), row(PLE_DIM)] + [_full_spec(a, 1) for a in fulls],
        out_specs=row(D_MODEL), out_shape=jax.ShapeDtypeStruct((t, D_MODEL), F32),
        compiler_params=_params(("arbitrary",)), name="moe_combine_postnorm_ple",
    )(x2d, y, y, meta, p2d, *fulls)


def kernel(x, p, rg_w_in, rg_conv_w, rg_conv_b, rg_w_a, rg_b_a, rg_w_x, rg_b_x, rg_lambda, rg_w_out,
           hg_w_in, hg_lb_logits, hg_w_out, ret_w_in, ret_w_out, gla_w_in, gla_w_gate, gla_b_gate,
           gla_w_out, dense_w_in, dense_w_out, moe_w_router, moe_w_in, moe_w_out, ple_w_proj,
           ple_w_gate, ln_mix_g, ln_mix_b, ln_ffn_g, ln_ffn_b):
    bsz, seq, d = x.shape
    t = bsz * seq
    depth = p.shape[0]
    moe_w_in_bf = moe_w_in.astype(BF16)
    moe_w_out_bf = moe_w_out.astype(BF16)
    lb_sm = jax.nn.softmax(hg_lb_logits.astype(F32), axis=0)
    lower_bounds = jnp.cumsum(lb_sm, axis=0) - lb_sm[0:1]
    h = x.reshape(t, d)
    for i in range(depth):
        kind, j = i % 4, i // 4
        if kind == 0:
            h = _rg_layer(h, bsz, seq, rg_w_in[j], rg_conv_w[j], rg_conv_b[j], rg_w_a[j], rg_b_a[j],
                          rg_w_x[j], rg_b_x[j], rg_lambda[j], rg_w_out[j], ln_mix_g[i], ln_mix_b[i])
        elif kind == 1:
            h = _hgrn2_layer(h, bsz, seq, hg_w_in[j], lower_bounds[i], hg_w_out[j], ln_mix_g[i], ln_mix_b[i])
        elif kind == 2:
            h = _ret_layer(h, bsz, seq, ret_w_in[j], ret_w_out[j], ln_mix_g[i], ln_mix_b[i])
        else:
            h = _gla_layer(h, bsz, seq, gla_w_in[j], gla_w_gate[j], gla_b_gate[j], gla_w_out[j],
                           ln_mix_g[i], ln_mix_b[i])
        p2d = p[i].reshape(t, p.shape[-1])
        if i % 2 == 0:
            h = _dense_ffn_layer(h, p2d, dense_w_in[i // 2], dense_w_out[i // 2], ln_ffn_g[i], ln_ffn_b[i],
                                 ple_w_proj[i], ple_w_gate[i])
        else:
            h = _moe_layer(h, p2d, moe_w_router[i // 2], moe_w_in_bf, moe_w_out_bf, i // 2,
                           ln_ffn_g[i], ln_ffn_b[i], ple_w_proj[i], ple_w_gate[i])
    return h.reshape(bsz, seq, d)
```
